```python
import jax, jax.numpy as jnp
from jax import lax
import numpy as np

D_MODEL = 1024
BATCH = 32
SEQ = 2048
DEPTH = 1
DEC_BATCH = 1
DEC_SEQ = 16384
PAST_LEN = 128

D_MIX = D_MODEL
GLA_HEADS = 4
GLA_DV = (D_MIX // 2) // GLA_HEADS
GLA_DK = GLA_DV // 2
GLA_RANK = 16
GLA_TAU = 16.0
GLA_CHUNK = 64
ATT_HEADS = 8
ATT_DH = (D_MIX // 2) // ATT_HEADS
DILATED = ((128, 1), (512, 4), (2048, 16))
D_FF = 2816
EPS = 1e-6
NEG = -1e30

GLA_QK_W = GLA_HEADS * GLA_DK
GLA_V_W = GLA_HEADS * GLA_DV
ATT_W = ATT_HEADS * ATT_DH
D_IN = 2 * GLA_QK_W + 2 * GLA_V_W + 2 * GLA_RANK + 3 * ATT_W

kernel_name = "hymba_gla_dilated_macaron_encoder"


def rmsnorm(x, g):
    xf = x.astype(jnp.float32)
    y = xf * lax.rsqrt(jnp.mean(xf * xf, axis=-1, keepdims=True) + EPS)
    return (y * g.astype(jnp.float32)).astype(x.dtype)


def swiglu(x, wg, wu, wd):
    return (jax.nn.silu(x @ wg) * (x @ wu)) @ wd


def gla_scan(q, k, v, log_a):
    B, L, H, DK = q.shape
    DV = v.shape[-1]
    C = GLA_CHUNK
    n = L // C
    q = q.reshape(B, n, C, H, DK)
    k = k.reshape(B, n, C, H, DK)
    v = v.reshape(B, n, C, H, DV)
    b = jnp.cumsum(log_a.reshape(B, n, C, H, DK), axis=2)
    q_dec = q * jnp.exp(b)
    k_inv = k * jnp.exp(-b)
    tril = jnp.tril(jnp.ones((C, C), dtype=bool))
    A = jnp.einsum('bnthk,bnshk->bnhts', q_dec, k_inv)
    A = jnp.where(tril, A, 0.0)
    o = jnp.einsum('bnhts,bnshv->bnthv', A, v)
    b_end = b[:, :, -1]
    kv = jnp.einsum('bnshk,bnshv->bnhkv', k * jnp.exp(b_end[:, :, None] - b), v)

    def step(S, inp):
        dec, kv_c = inp
        return dec[..., None] * S + kv_c, S

    S0 = jnp.zeros((B, H, DK, DV), jnp.float32)
    _, S_prev = lax.scan(step, S0, (jnp.moveaxis(jnp.exp(b_end), 1, 0), jnp.moveaxis(kv, 1, 0)))
    o = o + jnp.einsum('bnthk,nbhkv->bnthv', q_dec, S_prev)
    return o.reshape(B, L, H, DV)


def gla_mixer(q, k, v, r, lr_f, lr_b, w_a2_f, b_a_f, w_a2_b, b_a_b, g_out):
    B, L, _ = q.shape
    f32 = jnp.float32
    qf = q.reshape(B, L, GLA_HEADS, GLA_DK).astype(f32) * (GLA_DK ** -0.5)
    kf = k.reshape(B, L, GLA_HEADS, GLA_DK).astype(f32)
    vf = v.reshape(B, L, GLA_HEADS, GLA_DV).astype(f32)
    la_f = (jax.nn.log_sigmoid((lr_f @ w_a2_f + b_a_f).astype(f32)) / GLA_TAU).reshape(B, L, GLA_HEADS, GLA_DK)
    la_b = (jax.nn.log_sigmoid((lr_b @ w_a2_b + b_a_b).astype(f32)) / GLA_TAU).reshape(B, L, GLA_HEADS, GLA_DK)
    o_f = gla_scan(qf, kf, vf, la_f)
    flip = lambda t: jnp.flip(t, axis=1)
    o_b = flip(gla_scan(flip(qf), flip(kf), flip(vf), flip(la_b)))
    o = o_f + o_b
    o = o * lax.rsqrt(jnp.mean(o * o, axis=-1, keepdims=True) + EPS)
    o = o.reshape(B, L, GLA_V_W) * g_out.astype(f32)
    return (o * jax.nn.silu(r.astype(f32))).astype(q.dtype)


def alibi_slopes(h):
    return jnp.exp2(-8.0 * jnp.arange(1, h + 1, dtype=jnp.float32) / h)


def dilated_branch(q, k, v, window, dil, slopes):
    B, L, H, D = q.shape
    R = (window // 2) // dil
    n = L // dil
    nb = -(-n // R)
    npad = nb * R
    Bd = B * dil

    def to_sub(x):
        return x.reshape(B, n, dil, H, D).transpose(0, 2, 1, 3, 4).reshape(Bd, n, H, D)

    pad_q = ((0, 0), (0, npad - n), (0, 0), (0, 0))
    pad_k = ((0, 0), (R, npad - n + R), (0, 0), (0, 0))
    qb = jnp.pad(to_sub(q), pad_q).reshape(Bd, nb, R, H, D)

    def windows(x):
        xb = jnp.pad(to_sub(x), pad_k).reshape(Bd, nb + 2, R, H, D)
        return jnp.concatenate([xb[:, :-2], xb[:, 1:-1], xb[:, 2:]], axis=2)

    kw = windows(k)
    vw = windows(v).astype(jnp.float32)
    s = jnp.einsum('bjqhd,bjkhd->bjhqk', qb, kw).astype(jnp.float32) * (D ** -0.5)
    a = jnp.arange(R)
    kk = jnp.arange(3 * R)
    rel = kk[None, :] - R - a[:, None]
    key_idx = jnp.arange(nb)[:, None] * R + kk[None, :] - R
    valid = (jnp.abs(rel) <= R)[None] & ((key_idx >= 0) & (key_idx < n))[:, None, :]
    bias = -(slopes * dil)[:, None, None] * jnp.abs(rel).astype(jnp.float32)[None]
    s = jnp.where(valid[:, None], s + bias, NEG)
    m = jnp.max(s, axis=-1, keepdims=True)
    p = jnp.exp(s - m)
    den = jnp.sum(p, axis=-1)
    o = jnp.einsum('bjhqk,bjkhd->bjqhd', p, vw) / jnp.swapaxes(den, 2, 3)[..., None]
    lse = jnp.swapaxes(m[..., 0] + jnp.log(den), 2, 3)
    o = o.reshape(Bd, npad, H, D)[:, :n]
    lse = lse.reshape(Bd, npad, H)[:, :n]
    o = o.reshape(B, dil, n, H, D).transpose(0, 2, 1, 3, 4).reshape(B, L, H, D)
    lse = lse.reshape(B, dil, n, H).transpose(0, 2, 1, 3).reshape(B, L, H)
    return o, lse


def dilated_attention(q, k, v):
    B, L, _ = q.shape
    q = q.reshape(B, L, ATT_HEADS, ATT_DH)
    k = k.reshape(B, L, ATT_HEADS, ATT_DH)
    v = v.reshape(B, L, ATT_HEADS, ATT_DH)
    slopes = alibi_slopes(ATT_HEADS)
    outs, lses = [], []
    for window, dil in DILATED:
        o_i, l_i = dilated_branch(q, k, v, window, dil, slopes)
        outs.append(o_i)
        lses.append(l_i)
    alpha = jax.nn.softmax(jnp.stack(lses, axis=0), axis=0)
    o = jnp.sum(alpha[..., None] * jnp.stack(outs, axis=0), axis=0)
    return o.reshape(B, L, ATT_W).astype(q.dtype)


def layer(x, g_ffn1, w1_gate, w1_up, w1_down, g_mix, w_in, w_a2_fwd, b_a_fwd, w_a2_bwd, b_a_bwd,
          g_gla_out, w_out, g_ffn2, w2_gate, w2_up, w2_down):
    h = x + 0.5 * swiglu(rmsnorm(x, g_ffn1), w1_gate, w1_up, w1_down)
    z = rmsnorm(h, g_mix) @ w_in
    c = np.cumsum([0, GLA_QK_W, GLA_QK_W, GLA_V_W, GLA_V_W, GLA_RANK, GLA_RANK, ATT_W, ATT_W, ATT_W])
    gq, gk, gv, gr, lr_f, lr_b, aq, ak, av = [z[..., int(c[i]):int(c[i + 1])] for i in range(9)]
    o_gla = gla_mixer(gq, gk, gv, gr, lr_f, lr_b, w_a2_fwd, b_a_fwd, w_a2_bwd, b_a_bwd, g_gla_out)
    o_att = dilated_attention(aq, ak, av)
    h = h + jnp.concatenate([o_gla, o_att], axis=-1) @ w_out
    h = h + 0.5 * swiglu(rmsnorm(h, g_ffn2), w2_gate, w2_up, w2_down)
    return h


def setup_inputs(seed: int = 0) -> dict:
    key = jax.random.key(seed)
    ks = jax.random.split(key, 20)
    f32 = jnp.float32
    nrm = lambda k, shape, fan: jax.random.normal(k, shape, f32) * (fan ** -0.5)
    gain = lambda k, shape: 1.0 + 0.02 * jax.random.normal(k, shape, f32)
    return {
        "x_prompt": jax.random.normal(ks[0], (BATCH, SEQ, D_MODEL), f32),
        "x_sample": jax.random.normal(ks[1], (DEC_BATCH, DEC_SEQ, D_MODEL), f32),
        "g_ffn1": gain(ks[2], (DEPTH, D_MODEL)),
        "w1_gate": nrm(ks[3], (DEPTH, D_MODEL, D_FF), D_MODEL),
        "w1_up": nrm(ks[4], (DEPTH, D_MODEL, D_FF), D_MODEL),
        "w1_down": nrm(ks[5], (DEPTH, D_FF, D_MODEL), D_FF),
        "g_mix": gain(ks[6], (DEPTH, D_MODEL)),
        "w_in": nrm(ks[7], (DEPTH, D_MODEL, D_IN), D_MODEL),
        "w_a2_fwd": nrm(ks[8], (DEPTH, GLA_RANK, GLA_QK_W), GLA_RANK),
        "b_a_fwd": 0.1 * jax.random.normal(ks[9], (DEPTH, GLA_QK_W), f32),
        "w_a2_bwd": nrm(ks[10], (DEPTH, GLA_RANK, GLA_QK_W), GLA_RANK),
        "b_a_bwd": 0.1 * jax.random.normal(ks[11], (DEPTH, GLA_QK_W), f32),
        "g_gla_out": gain(ks[12], (DEPTH, GLA_V_W)),
        "w_out": nrm(ks[13], (DEPTH, D_MIX, D_MODEL), D_MIX),
        "g_ffn2": gain(ks[14], (DEPTH, D_MODEL)),
        "w2_gate": nrm(ks[15], (DEPTH, D_MODEL, D_FF), D_MODEL),
        "w2_up": nrm(ks[16], (DEPTH, D_MODEL, D_FF), D_MODEL),
        "w2_down": nrm(ks[17], (DEPTH, D_FF, D_MODEL), D_FF),
        "g_final": gain(ks[18], (D_MODEL,)),
    }


def reference(x_prompt, x_sample, g_ffn1, w1_gate, w1_up, w1_down, g_mix, w_in, w_a2_fwd, b_a_fwd,
              w_a2_bwd, b_a_bwd, g_gla_out, w_out, g_ffn2, w2_gate, w2_up, w2_down, g_final):
    def trunk(x):
        for l in range(DEPTH):
            x = layer(x, g_ffn1[l], w1_gate[l], w1_up[l], w1_down[l], g_mix[l], w_in[l],
                      w_a2_fwd[l], b_a_fwd[l], w_a2_bwd[l], b_a_bwd[l], g_gla_out[l], w_out[l],
                      g_ffn2[l], w2_gate[l], w2_up[l], w2_down[l])
        return rmsnorm(x, g_final)

    y_prompt = trunk(x_prompt)
    y_sample = trunk(x_sample)
    return (y_prompt, y_sample)
```

```python
import functools

import jax
import jax.numpy as jnp
from jax import lax
from jax.experimental import pallas as pl
from jax.experimental.pallas import tpu as pltpu

f32 = jnp.float32
bf16 = jnp.bfloat16

D_MODEL = 1024
D_FF = 2816
GLA_HEADS = 4
GLA_DK = 64
GLA_DV = 128
GLA_RANK = 16
GLA_TAU = 16.0
GLA_CHUNK = 64
GLA_QK_W = GLA_HEADS * GLA_DK
GLA_V_W = GLA_HEADS * GLA_DV
ATT_HEADS = 8
ATT_DH = 64
ATT_W = ATT_HEADS * ATT_DH
ATT_RADIUS = 64
DILATIONS = (1, 4, 16)
EPS = 1e-6
NEG = -1e30

LANES = 128
TM = 512
FF_CHUNK = 256
TL = 512
TB = 512
TQ = 128
KW = TQ + 2 * ATT_RADIUS
VMEM_LIMIT = 56 * 1024 * 1024

LR_PAD = LANES
Z_GLA_W = 2 * GLA_QK_W + 2 * GLA_V_W + LR_PAD
Z_ATT_W = 3 * ATT_W

_NT = (((1,), (1,)), ((), ()))
_TN = (((0,), (0,)), ((), ()))


def _dot(a, b):
    return jnp.dot(a, b, preferred_element_type=f32)


def _rms(x, g):
    return x * lax.rsqrt(jnp.mean(x * x, axis=-1, keepdims=True) + EPS) * g


def _split(x):
    hi = x.astype(bf16)
    lo = (x - hi.astype(f32)).astype(bf16)
    return hi, lo


def _const_spec(shape):
    zeros = (0,) * len(shape)
    return pl.BlockSpec(shape, lambda *_: zeros, pipeline_mode=pl.Buffered(1))


def _params(*sem):
    return pltpu.CompilerParams(dimension_semantics=sem, vmem_limit_bytes=VMEM_LIMIT)


def _ffn_kernel(x_ref, g_ref, wg_ref, wu_ref, wd_ref, *rest, final):
    if final:
        gf_ref, o_ref = rest
    else:
        (o_ref,) = rest
    x = x_ref[0]
    xn = _rms(x, g_ref[...]).astype(bf16)
    acc = jnp.zeros(x.shape, f32)
    for c in range(D_FF // FF_CHUNK):
        sl = slice(c * FF_CHUNK, (c + 1) * FF_CHUNK)
        g = _dot(xn, wg_ref[:, sl])
        u = _dot(xn, wu_ref[:, sl])
        a = (g * (1.0 / (1.0 + jnp.exp(-g))) * u).astype(bf16)
        acc = acc + _dot(a, wd_ref[sl, :])
    h = x + 0.5 * acc
    if final:
        h = _rms(h, gf_ref[...])
    o_ref[0] = h


def _ffn(x, g, wg, wu, wd, g_final=None):
    B, L, D = x.shape
    final = g_final is not None
    tok = pl.BlockSpec((1, TM, D), lambda b, t: (b, t, 0))
    in_specs = [tok, _const_spec((1, D)), _const_spec((D, D_FF)), _const_spec((D, D_FF)),
                _const_spec((D_FF, D))]
    args = [x, g, wg, wu, wd]
    if final:
        in_specs.append(_const_spec((1, D)))
        args.append(g_final)
    return pl.pallas_call(
        functools.partial(_ffn_kernel, final=final),
        grid=(B, L // TM), in_specs=in_specs, out_specs=tok,
        out_shape=jax.ShapeDtypeStruct((B, L, D), f32),
        compiler_params=_params("parallel", "parallel"),
        name="ffn_final" if final else "ffn",
    )(*args)


def _proj_kernel(h_ref, g_ref, wg_ref, wa_ref, whi_ref, wlo_ref, ba_ref,
                 qk_ref, v_ref, r_ref, la_ref, q1_ref, kv1_ref, q4_ref, kv4_ref, q16_ref, kv16_ref,
                 zs_ref):
    un = _rms(h_ref[0], g_ref[...]).astype(bf16)
    zg = _dot(un, wg_ref[...])
    qk_ref[0] = zg[:, :2 * GLA_QK_W].astype(bf16)
    v_ref[0] = zg[:, 2 * GLA_QK_W:2 * GLA_QK_W + GLA_V_W].astype(bf16)
    r_ref[0] = zg[:, 2 * GLA_QK_W + GLA_V_W:2 * GLA_QK_W + 2 * GLA_V_W].astype(bf16)
    lr_hi, lr_lo = _split(zg[:, Z_GLA_W - LR_PAD:])
    pre = _dot(lr_hi, whi_ref[...]) + _dot(lr_hi, wlo_ref[...]) + _dot(lr_lo, whi_ref[...]) + ba_ref[...]
    la_ref[0] = (jnp.minimum(pre, 0.0) - jnp.log1p(jnp.exp(-jnp.abs(pre)))) * (1.0 / GLA_TAU)

    za = _dot(un, wa_ref[...])
    q1_ref[0] = za[:, :ATT_W].astype(bf16)
    kv1_ref[0] = za[:, ATT_W:].astype(bf16)
    ngrp = Z_ATT_W // LANES
    for gi in range(ngrp):
        zs_ref[gi] = za[:, gi * LANES:(gi + 1) * LANES]
    for d, q_ref, kv_ref in ((4, q4_ref, kv4_ref), (16, q16_ref, kv16_ref)):
        for c in range(d):
            for gi in range(ngrp):
                rows = zs_ref[gi, pl.ds(c, TM // d, stride=d), :].astype(bf16)
                col = gi * LANES
                if col < ATT_W:
                    q_ref[0, c, :, col:col + LANES] = rows
                else:
                    kv_ref[0, c, :, col - ATT_W:col - ATT_W + LANES] = rows


def _proj(h, g, w_gla, w_att, wa_hi, wa_lo, ba):
    B, L, D = h.shape
    tok = lambda w: pl.BlockSpec((1, TM, w), lambda b, t: (b, t, 0))
    cls = lambda d, w: pl.BlockSpec((1, d, TM // d, w), lambda b, t: (b, 0, t, 0))
    sds = jax.ShapeDtypeStruct
    out_shape = [sds((B, L, 2 * GLA_QK_W), bf16), sds((B, L, GLA_V_W), bf16), sds((B, L, GLA_V_W), bf16),
                 sds((B, L, 2 * GLA_QK_W), f32),
                 sds((B, L, ATT_W), bf16), sds((B, L, 2 * ATT_W), bf16)]
    out_specs = [tok(2 * GLA_QK_W), tok(GLA_V_W), tok(GLA_V_W), tok(2 * GLA_QK_W), tok(ATT_W), tok(2 * ATT_W)]
    for d in DILATIONS[1:]:
        out_shape += [sds((B, d, L // d, ATT_W), bf16), sds((B, d, L // d, 2 * ATT_W), bf16)]
        out_specs += [cls(d, ATT_W), cls(d, 2 * ATT_W)]
    return pl.pallas_call(
        _proj_kernel,
        grid=(B, L // TM),
        in_specs=[tok(D), _const_spec((1, D)), _const_spec((D, Z_GLA_W)), _const_spec((D, Z_ATT_W)),
                  _const_spec((LR_PAD, 2 * GLA_QK_W)), _const_spec((LR_PAD, 2 * GLA_QK_W)),
                  _const_spec((1, 2 * GLA_QK_W))],
        out_specs=out_specs, out_shape=out_shape,
        scratch_shapes=[pltpu.VMEM((Z_ATT_W // LANES, TM, LANES), f32)],
        compiler_params=_params("parallel", "parallel"),
        name="proj",
    )(h, g, w_gla, w_att, wa_hi, wa_lo, ba)


def _gla_kernel(qk_ref, v_ref, la_ref, *rest, reverse, final):
    if final:
        of_ref, r_ref, go_ref, o_ref, st_ref = rest
    else:
        o_ref, st_ref = rest
    C = GLA_CHUNK

    @pl.when(pl.program_id(1) == 0)
    def _():
        st_ref[...] = jnp.zeros(st_ref.shape, f32)

    row = lax.broadcasted_iota(jnp.int32, (C, C), 0)
    col = lax.broadcasted_iota(jnp.int32, (C, C), 1)
    tri = (col >= row) if reverse else (col <= row)
    tri_b = tri.astype(bf16)
    half = lax.broadcasted_iota(jnp.int32, (C, LANES), 1) // GLA_DK
    end = 0 if reverse else C - 1
    order = range(TL // C - 1, -1, -1) if reverse else range(TL // C)
    for ci in order:
        rs = slice(ci * C, (ci + 1) * C)
        la_hi, la_lo = _split(la_ref[0, rs, :])
        b = _dot(tri_b, la_hi) + _dot(tri_b, la_lo)
        b_end = b[end:end + 1, :]
        qk = qk_ref[0, rs, :].astype(f32)
        q, k = qk[:, :GLA_QK_W], qk[:, GLA_QK_W:]
        q_dec = (q * jnp.exp(b)).astype(bf16)
        k_inv = (k * jnp.exp(-b)).astype(bf16)
        k_end = (k * jnp.exp(b_end - b)).astype(bf16)
        dec = jnp.exp(b_end)
        for h in range(GLA_HEADS):
            ps = slice((h // 2) * LANES, (h // 2 + 1) * LANES)
            qd = jnp.where(half == (h % 2), q_dec[:, ps], jnp.zeros((), bf16))
            a = lax.dot_general(qd, k_inv[:, ps], _NT, preferred_element_type=f32)
            a = jnp.where(tri, a, 0.0).astype(bf16)
            vh = v_ref[0, rs, h * GLA_DV:(h + 1) * GLA_DV]
            st = st_ref[h]
            o = _dot(a, vh) + lax.dot_general(qd, st.astype(bf16), _NT, preferred_element_type=f32)
            st_ref[h] = st * dec[:, ps] + lax.dot_general(vh, k_end[:, ps], _TN, preferred_element_type=f32)
            vs = slice(h * GLA_DV, (h + 1) * GLA_DV)
            if final:
                o = o + of_ref[0, rs, vs]
                o = o * lax.rsqrt(jnp.mean(o * o, axis=-1, keepdims=True) + EPS) * go_ref[:, vs]
                r = r_ref[0, rs, vs].astype(f32)
                o_ref[0, rs, vs] = (o * (r * (1.0 / (1.0 + jnp.exp(-r))))).astype(bf16)
            else:
                o_ref[0, rs, vs] = o


def _gla(qk, v, la, r, g_out):
    B, L, _ = qk.shape
    nblk = L // TL
    scratch = [pltpu.VMEM((GLA_HEADS, GLA_DV, LANES), f32)]
    fwd = lambda w, cb=0: pl.BlockSpec((1, TL, w), lambda b, j: (b, j, cb))
    bwd = lambda w, cb=0: pl.BlockSpec((1, TL, w), lambda b, j: (b, nblk - 1 - j, cb))
    o_f = pl.pallas_call(
        functools.partial(_gla_kernel, reverse=False, final=False),
        grid=(B, nblk),
        in_specs=[fwd(2 * GLA_QK_W), fwd(GLA_V_W), fwd(GLA_QK_W, 0)],
        out_specs=fwd(GLA_V_W), out_shape=jax.ShapeDtypeStruct((B, L, GLA_V_W), f32),
        scratch_shapes=scratch, compiler_params=_params("parallel", "arbitrary"),
        name="gla_fwd",
    )(qk, v, la)
    return pl.pallas_call(
        functools.partial(_gla_kernel, reverse=True, final=True),
        grid=(B, nblk),
        in_specs=[bwd(2 * GLA_QK_W), bwd(GLA_V_W), bwd(GLA_QK_W, 1), bwd(GLA_V_W), bwd(GLA_V_W),
                  _const_spec((1, GLA_V_W))],
        out_specs=bwd(GLA_V_W), out_shape=jax.ShapeDtypeStruct((B, L, GLA_V_W), bf16),
        scratch_shapes=scratch, compiler_params=_params("parallel", "arbitrary"),
        name="gla_bwd",
    )(qk, v, la, o_f, r, g_out)


def _att_kernel(q_ref, kv_ref, kvp_ref, kvn_ref, o_ref, lse_ref, ext_ref, bias_ref, *, n_seq, dil):
    i = pl.program_id(0)
    R = ATT_RADIUS

    @pl.when(i == 0)
    def _():
        row = lax.broadcasted_iota(jnp.int32, (TQ, KW), 0)
        col = lax.broadcasted_iota(jnp.int32, (TQ, KW), 1)
        rel = jnp.abs(col - R - row)
        for h in range(ATT_HEADS):
            slope = 2.0 ** (-8.0 * (h + 1) / ATT_HEADS) * dil
            bias_ref[h] = jnp.where(rel <= R, -slope * rel.astype(f32), NEG)

    ext_ref[0:R] = kvp_ref[...]
    ext_ref[R:R + TB] = kv_ref[...]
    ext_ref[R + TB:] = kvn_ref[...]

    colk = lax.broadcasted_iota(jnp.int32, (1, KW), 1)
    half = lax.broadcasted_iota(jnp.int32, (TQ, LANES), 1) // ATT_DH
    lane = lax.broadcasted_iota(jnp.int32, (TQ, LANES), 1)
    for a in range(TB // TQ):
        start = i * TB + a * TQ
        lo = jnp.where((start % n_seq) == 0, R, 0)
        hi = jnp.where(((start + TQ) % n_seq) == 0, R + TQ, KW)
        edge = jnp.where((colk < lo) | (colk >= hi), NEG, 0.0)
        lse_all = jnp.zeros((TQ, LANES), f32)
        for p in range(ATT_HEADS // 2):
            ps = slice(p * LANES, (p + 1) * LANES)
            qp = q_ref[a * TQ:(a + 1) * TQ, ps]
            kp = ext_ref[a * TQ:a * TQ + KW, ps]
            vp = ext_ref[a * TQ:a * TQ + KW, ATT_W + p * LANES:ATT_W + (p + 1) * LANES]
            outs = []
            for e in range(2):
                h = 2 * p + e
                qm = jnp.where(half == e, qp, jnp.zeros((), bf16))
                s = lax.dot_general(qm, kp, _NT, preferred_element_type=f32) + bias_ref[h] + edge
                m = jnp.max(s, axis=-1, keepdims=True)
                pe = jnp.exp(s - m)
                l = jnp.sum(pe, axis=-1, keepdims=True)
                outs.append(_dot(pe.astype(bf16), vp) / l)
                lse_all = jnp.where(lane == h, m + jnp.log(l), lse_all)
            o_ref[a * TQ:(a + 1) * TQ, ps] = jnp.where(half == 0, outs[0], outs[1]).astype(bf16)
        lse_ref[a * TQ:(a + 1) * TQ, :] = lse_all


def _att(q, kv, n_seq, dil):
    rows = q.shape[0]
    nhalo = rows // ATT_RADIUS
    per = TB // ATT_RADIUS
    return pl.pallas_call(
        functools.partial(_att_kernel, n_seq=n_seq, dil=dil),
        grid=(rows // TB,),
        in_specs=[pl.BlockSpec((TB, ATT_W), lambda i: (i, 0)),
                  pl.BlockSpec((TB, 2 * ATT_W), lambda i: (i, 0)),
                  pl.BlockSpec((ATT_RADIUS, 2 * ATT_W), lambda i: (jnp.maximum(i * per - 1, 0), 0)),
                  pl.BlockSpec((ATT_RADIUS, 2 * ATT_W), lambda i: (jnp.minimum((i + 1) * per, nhalo - 1), 0))],
        out_specs=[pl.BlockSpec((TB, ATT_W), lambda i: (i, 0)), pl.BlockSpec((TB, LANES), lambda i: (i, 0))],
        out_shape=[jax.ShapeDtypeStruct((rows, ATT_W), bf16), jax.ShapeDtypeStruct((rows, LANES), f32)],
        scratch_shapes=[pltpu.VMEM((TB + 2 * ATT_RADIUS, 2 * ATT_W), bf16),
                        pltpu.VMEM((ATT_HEADS, TQ, KW), f32)],
        compiler_params=_params("arbitrary"),
        name=f"att_d{dil}",
    )(q, kv, kv, kv)


def _mix_kernel(h_ref, og_ref, o1_ref, l1_ref, o4_ref, l4_ref, o16_ref, l16_ref, wo_ref, out_ref,
                os_ref, ls_ref):
    ngrp = ATT_W // LANES

    def natural(o_ref, l_ref, d):
        for c in range(d):
            ls_ref[0, pl.ds(c, TM // d, stride=d), :] = l_ref[0, c]
            oc = o_ref[0, c].astype(f32)
            for gi in range(ngrp):
                os_ref[gi, pl.ds(c, TM // d, stride=d), :] = oc[:, gi * LANES:(gi + 1) * LANES]
        o = jnp.concatenate([os_ref[gi] for gi in range(ngrp)], axis=-1)
        return o, ls_ref[0]

    o1, l1 = o1_ref[0].astype(f32), l1_ref[0]
    o4, l4 = natural(o4_ref, l4_ref, 4)
    o16, l16 = natural(o16_ref, l16_ref, 16)
    m = jnp.maximum(jnp.maximum(l1, l4), l16)
    w1, w4, w16 = jnp.exp(l1 - m), jnp.exp(l4 - m), jnp.exp(l16 - m)
    inv = 1.0 / (w1 + w4 + w16)
    er = lax.broadcasted_iota(jnp.int32, (LANES, ATT_W), 0)
    ec = lax.broadcasted_iota(jnp.int32, (LANES, ATT_W), 1)
    expand = (ec // ATT_DH == er).astype(bf16)
    o = jnp.zeros((TM, ATT_W), f32)
    for w, ob in ((w1, o1), (w4, o4), (w16, o16)):
        hi, lo = _split(w * inv)
        o = o + (_dot(hi, expand) + _dot(lo, expand)) * ob
    mixed = _dot(og_ref[0], wo_ref[:GLA_V_W, :]) + _dot(o.astype(bf16), wo_ref[GLA_V_W:, :])
    out_ref[0] = h_ref[0] + mixed


def _mix(h, og, o1, l1, o4, l4, o16, l16, w_out):
    B, L, D = h.shape
    tok = lambda w: pl.BlockSpec((1, TM, w), lambda b, t: (b, t, 0))
    cls = lambda d, w: pl.BlockSpec((1, d, TM // d, w), lambda b, t: (b, 0, t, 0))
    return pl.pallas_call(
        _mix_kernel,
        grid=(B, L // TM),
        in_specs=[tok(D), tok(GLA_V_W), tok(ATT_W), tok(LANES), cls(4, ATT_W), cls(4, LANES),
                  cls(16, ATT_W), cls(16, LANES), _const_spec((D, D))],
        out_specs=tok(D), out_shape=jax.ShapeDtypeStruct((B, L, D), f32),
        scratch_shapes=[pltpu.VMEM((ATT_W // LANES, TM, LANES), f32), pltpu.VMEM((1, TM, LANES), f32)],
        compiler_params=_params("parallel", "parallel"),
        name="mix",
    )(h, og, o1, l1, o4, l4, o16, l16, w_out)


def _prep_layer(g_ffn1, w1_gate, w1_up, w1_down, g_mix, w_in, w_a2_fwd, b_a_fwd, w_a2_bwd, b_a_bwd,
                g_gla_out, w_out, g_ffn2, w2_gate, w2_up, w2_down):
    c = [0]
    for w in (GLA_QK_W, GLA_QK_W, GLA_V_W, GLA_V_W, GLA_RANK, GLA_RANK, ATT_W, ATT_W, ATT_W):
        c.append(c[-1] + w)
    gq, gk, gv, gr, lrf, lrb, aq, ak, av = [w_in[:, c[i]:c[i + 1]] for i in range(9)]
    pad = jnp.zeros((D_MODEL, LR_PAD - 2 * GLA_RANK), f32)
    w_gla = jnp.concatenate([gq * GLA_DK ** -0.5, gk, gv, gr, lrf, lrb, pad], axis=1).astype(bf16)
    w_att = jnp.concatenate([aq * ATT_DH ** -0.5, ak, av], axis=1).astype(bf16)
    wa = jnp.zeros((LR_PAD, 2 * GLA_QK_W), f32)
    wa = wa.at[:GLA_RANK, :GLA_QK_W].set(w_a2_fwd).at[GLA_RANK:2 * GLA_RANK, GLA_QK_W:].set(w_a2_bwd)
    wa_hi = wa.astype(bf16)
    wa_lo = (wa - wa_hi.astype(f32)).astype(bf16)
    ba = jnp.concatenate([b_a_fwd, b_a_bwd])[None, :]
    return dict(
        ffn1=(g_ffn1[None, :], w1_gate.astype(bf16), w1_up.astype(bf16), w1_down.astype(bf16)),
        proj=(g_mix[None, :], w_gla, w_att, wa_hi, wa_lo, ba),
        g_out=g_gla_out[None, :], w_out=w_out.astype(bf16),
        ffn2=(g_ffn2[None, :], w2_gate.astype(bf16), w2_up.astype(bf16), w2_down.astype(bf16)))


def _layer(x, p, g_final):
    B, L, _ = x.shape
    h = _ffn(x, *p["ffn1"])
    qk, v, r, la, q1, kv1, q4, kv4, q16, kv16 = _proj(h, *p["proj"])
    og = _gla(qk, v, la, r, p["g_out"])
    outs = []
    for d, q, kv in ((1, q1, kv1), (4, q4, kv4), (16, q16, kv16)):
        o, lse = _att(q.reshape(B * L, ATT_W), kv.reshape(B * L, 2 * ATT_W), L // d, d)
        shape = (B, L) if d == 1 else (B, d, L // d)
        outs += [o.reshape(*shape, ATT_W), lse.reshape(*shape, LANES)]
    h = _mix(h, og, *outs, p["w_out"])
    return _ffn(h, *p["ffn2"], g_final=g_final)


def kernel(x_prompt, x_sample, g_ffn1, w1_gate, w1_up, w1_down, g_mix, w_in, w_a2_fwd, b_a_fwd, w_a2_bwd, b_a_bwd, g_gla_out, w_out, g_ffn2, w2_gate, w2_up, w2_down, g_final):
    depth = g_ffn1.shape[0]
    assert depth == 1, "final norm is fused into the last layer's second FFN"
    p = _prep_layer(g_ffn1[0], w1_gate[0], w1_up[0], w1_down[0], g_mix[0], w_in[0], w_a2_fwd[0], b_a_fwd[0],
                    w_a2_bwd[0], b_a_bwd[0], g_gla_out[0], w_out[0], g_ffn2[0], w2_gate[0], w2_up[0], w2_down[0])
    gf = g_final[None, :]
    return (_layer(x_prompt, p, gf), _layer(x_sample, p, gf))
```

```python
import functools

import jax
import jax.numpy as jnp
from jax import lax
from jax.experimental import pallas as pl
from jax.experimental.pallas import tpu as pltpu

f32 = jnp.float32
bf16 = jnp.bfloat16

D_MODEL = 1024
D_FF = 2816
GLA_HEADS = 4
GLA_DK = 64
GLA_DV = 128
GLA_RANK = 16
GLA_TAU = 16.0
GLA_CHUNK = 64
GLA_QK_W = GLA_HEADS * GLA_DK
GLA_V_W = GLA_HEADS * GLA_DV
ATT_HEADS = 8
ATT_DH = 64
ATT_W = ATT_HEADS * ATT_DH
ATT_RADIUS = 64
DILATIONS = (1, 4, 16)
EPS = 1e-6
NEG = -1e30

LOG2E = 1.4426950408889634
LN2 = 0.6931471805599453

LANES = 128
MXU_N = 256
TM = 512
FF_CHUNK = MXU_N
TL = 512
TB = 512
TQ = 128
KW = TQ + 2 * ATT_RADIUS
GLA_LAG_L = 2
GLA_LAG_C = 2
ATT_LAG_P = 2
ATT_LAG_V = 2
VMEM_LIMIT = 56 * 1024 * 1024

LR_PAD = LANES
Z_GLA_W = 2 * GLA_QK_W + 2 * GLA_V_W + LR_PAD
Z_ATT_W = 3 * ATT_W

_NT = (((1,), (1,)), ((), ()))
_TN = (((0,), (0,)), ((), ()))


def _dot(a, b):
    return jnp.dot(a, b, preferred_element_type=f32)


def _rms(x, g):
    return x * lax.rsqrt(jnp.mean(x * x, axis=-1, keepdims=True) + EPS) * g


def _split(x):
    hi = x.astype(bf16)
    lo = (x - hi.astype(f32)).astype(bf16)
    return hi, lo


def _const_spec(shape):
    zeros = (0,) * len(shape)
    return pl.BlockSpec(shape, lambda *_: zeros, pipeline_mode=pl.Buffered(1))


def _params(*sem):
    return pltpu.CompilerParams(dimension_semantics=sem, vmem_limit_bytes=VMEM_LIMIT)


def _ffn_kernel(x_ref, g_ref, wg_ref, wu_ref, wd_ref, *rest, final):
    if final:
        gf_ref, o_ref = rest
    else:
        (o_ref,) = rest
    x = x_ref[0]
    xn = _rms(x, g_ref[...]).astype(bf16)
    acc = jnp.zeros(x.shape, f32)
    for c in range(D_FF // FF_CHUNK):
        sl = slice(c * FF_CHUNK, (c + 1) * FF_CHUNK)
        g = _dot(xn, wg_ref[:, sl])
        u = _dot(xn, wu_ref[:, sl])
        a = (g * (1.0 / (1.0 + jnp.exp(-g))) * u).astype(bf16)
        acc = acc + _dot(a, wd_ref[sl, :])
    h = x + 0.5 * acc
    if final:
        h = _rms(h, gf_ref[...])
    o_ref[0] = h


def _ffn(x, g, wg, wu, wd, g_final=None):
    B, L, D = x.shape
    final = g_final is not None
    tok = pl.BlockSpec((1, TM, D), lambda b, t: (b, t, 0))
    in_specs = [tok, _const_spec((1, D)), _const_spec((D, D_FF)), _const_spec((D, D_FF)),
                _const_spec((D_FF, D))]
    args = [x, g, wg, wu, wd]
    if final:
        in_specs.append(_const_spec((1, D)))
        args.append(g_final)
    return pl.pallas_call(
        functools.partial(_ffn_kernel, final=final),
        grid=(B, L // TM), in_specs=in_specs, out_specs=tok,
        out_shape=jax.ShapeDtypeStruct((B, L, D), f32),
        compiler_params=_params("parallel", "parallel"),
        name="ffn_final" if final else "ffn",
    )(*args)


def _proj_kernel(h_ref, g_ref, wg_ref, wa_ref, whi_ref, wlo_ref, ba_ref,
                 qk_ref, v_ref, r_ref, la_ref, q1_ref, kv1_ref, q4_ref, kv4_ref, q16_ref, kv16_ref,
                 zs_ref):
    un = _rms(h_ref[0], g_ref[...]).astype(bf16)
    ngrp = Z_ATT_W // LANES
    per = MXU_N // LANES

    def att_cols(c):
        z = _dot(un, wa_ref[:, c * MXU_N:(c + 1) * MXU_N])
        if c * MXU_N < ATT_W:
            q1_ref[0, :, c * MXU_N:(c + 1) * MXU_N] = z.astype(bf16)
        else:
            kv1_ref[0, :, c * MXU_N - ATT_W:(c + 1) * MXU_N - ATT_W] = z.astype(bf16)
        for k in range(per):
            zs_ref[c * per + k] = z[:, k * LANES:(k + 1) * LANES]

    def permute(d, q_ref, kv_ref, classes):
        for c in classes:
            for gi in range(ngrp):
                rows = zs_ref[gi, pl.ds(c, TM // d, stride=d), :].astype(bf16)
                col = gi * LANES
                if col < ATT_W:
                    q_ref[0, c, :, col:col + LANES] = rows
                else:
                    kv_ref[0, c, :, col - ATT_W:col - ATT_W + LANES] = rows

    def gla_cols(c):
        ref, off = ((qk_ref, 0), (qk_ref, MXU_N), (v_ref, 0), (v_ref, MXU_N), (r_ref, 0), (r_ref, MXU_N))[c]
        ref[0, :, off:off + MXU_N] = _dot(un, wg_ref[:, c * MXU_N:(c + 1) * MXU_N]).astype(bf16)

    lr_hi, lr_lo = _split(_dot(un, wg_ref[:, Z_GLA_W - LR_PAD:]))
    att_cols(0)
    pre = _dot(lr_hi, whi_ref[...]) + _dot(lr_hi, wlo_ref[...]) + _dot(lr_lo, whi_ref[...]) + ba_ref[...]
    att_cols(1)
    att_cols(2)
    la_ref[0] = (jnp.minimum(pre, 0.0) - jnp.log(1.0 + jnp.exp(-jnp.abs(pre)))) * (1.0 / GLA_TAU)
    for c in range(3, Z_ATT_W // MXU_N):
        att_cols(c)
    gla_cols(0)
    permute(4, q4_ref, kv4_ref, range(0, 2))
    gla_cols(1)
    permute(4, q4_ref, kv4_ref, range(2, 4))
    for c in range(2, 6):
        gla_cols(c)
        permute(16, q16_ref, kv16_ref, range((c - 2) * 4, (c - 1) * 4))


def _proj(h, g, w_gla, w_att, wa_hi, wa_lo, ba):
    B, L, D = h.shape
    tok = lambda w: pl.BlockSpec((1, TM, w), lambda b, t: (b, t, 0))
    cls = lambda d, w: pl.BlockSpec((1, d, TM // d, w), lambda b, t: (b, 0, t, 0))
    sds = jax.ShapeDtypeStruct
    out_shape = [sds((B, L, 2 * GLA_QK_W), bf16), sds((B, L, GLA_V_W), bf16), sds((B, L, GLA_V_W), bf16),
                 sds((B, L, 2 * GLA_QK_W), f32),
                 sds((B, L, ATT_W), bf16), sds((B, L, 2 * ATT_W), bf16)]
    out_specs = [tok(2 * GLA_QK_W), tok(GLA_V_W), tok(GLA_V_W), tok(2 * GLA_QK_W), tok(ATT_W), tok(2 * ATT_W)]
    for d in DILATIONS[1:]:
        out_shape += [sds((B, d, L // d, ATT_W), bf16), sds((B, d, L // d, 2 * ATT_W), bf16)]
        out_specs += [cls(d, ATT_W), cls(d, 2 * ATT_W)]
    return pl.pallas_call(
        _proj_kernel,
        grid=(B, L // TM),
        in_specs=[tok(D), _const_spec((1, D)), _const_spec((D, Z_GLA_W)), _const_spec((D, Z_ATT_W)),
                  _const_spec((LR_PAD, 2 * GLA_QK_W)), _const_spec((LR_PAD, 2 * GLA_QK_W)),
                  _const_spec((1, 2 * GLA_QK_W))],
        out_specs=out_specs, out_shape=out_shape,
        scratch_shapes=[pltpu.VMEM((Z_ATT_W // LANES, TM, LANES), f32)],
        compiler_params=_params("parallel", "parallel"),
        name="proj",
    )(h, g, w_gla, w_att, wa_hi, wa_lo, ba)


def _gla_kernel(qk_ref, v_ref, la_ref, *rest, reverse, final):
    if final:
        of_ref, r_ref, go_ref, o_ref, st_ref = rest
    else:
        o_ref, st_ref = rest
    C = GLA_CHUNK

    @pl.when(pl.program_id(1) == 0)
    def _():
        st_ref[...] = jnp.zeros(st_ref.shape, f32)

    row = lax.broadcasted_iota(jnp.int32, (C, C), 0)
    col = lax.broadcasted_iota(jnp.int32, (C, C), 1)
    tri = (col >= row) if reverse else (col <= row)
    tri_b = tri.astype(bf16)
    half = lax.broadcasted_iota(jnp.int32, (C, LANES), 1) // GLA_DK
    end = 0 if reverse else C - 1
    order = list(range(TL // C - 1, -1, -1) if reverse else range(TL // C))
    pair = lambda h: slice((h // 2) * LANES, (h // 2 + 1) * LANES)

    def decays(ci):
        rs = slice(ci * C, (ci + 1) * C)
        la_hi, la_lo = _split(la_ref[0, rs, :])
        b = _dot(tri_b, la_hi) + _dot(tri_b, la_lo)
        b_end = b[end:end + 1, :]
        qk = qk_ref[0, rs, :].astype(f32)
        q, k = qk[:, :GLA_QK_W], qk[:, GLA_QK_W:]
        q_dec = (q * jnp.exp(b)).astype(bf16)
        k_inv = (k * jnp.exp(-b)).astype(bf16)
        k_end = (k * jnp.exp(b_end - b)).astype(bf16)
        return q_dec, k_inv, k_end, jnp.exp(b_end)

    def local(ci, q_dec, k_inv, k_end, dec):
        rs = slice(ci * C, (ci + 1) * C)
        out = []
        for h in range(GLA_HEADS):
            qd = jnp.where(half == (h % 2), q_dec[:, pair(h)], jnp.zeros((), bf16))
            a = lax.dot_general(qd, k_inv[:, pair(h)], _NT, preferred_element_type=f32)
            a = jnp.where(tri, a, 0.0).astype(bf16)
            vh = v_ref[0, rs, h * GLA_DV:(h + 1) * GLA_DV]
            kv = lax.dot_general(vh, k_end[:, pair(h)], _TN, preferred_element_type=f32)
            out.append((qd, _dot(a, vh), kv))
        return out

    def carry(ci, heads, dec, st):
        rs = slice(ci * C, (ci + 1) * C)
        for h, (qd, o, kv) in enumerate(heads):
            o = o + lax.dot_general(qd, st[h].astype(bf16), _NT, preferred_element_type=f32)
            st[h] = st[h] * dec[:, pair(h)] + kv
            vs = slice(h * GLA_DV, (h + 1) * GLA_DV)
            if final:
                o = o + of_ref[0, rs, vs]
                o = o * lax.rsqrt(jnp.mean(o * o, axis=-1, keepdims=True) + EPS) * go_ref[:, vs]
                r = r_ref[0, rs, vs].astype(f32)
                o_ref[0, rs, vs] = (o * (r * (1.0 / (1.0 + jnp.exp(-r))))).astype(bf16)
            else:
                o_ref[0, rs, vs] = o

    st = [st_ref[h] for h in range(GLA_HEADS)]
    dc, lc = {}, {}
    for t in range(len(order) + GLA_LAG_L + GLA_LAG_C):
        if t < len(order):
            dc[t] = decays(order[t])
        n = t - GLA_LAG_L
        if 0 <= n < len(order):
            lc[n] = local(order[n], *dc[n])
        n = t - GLA_LAG_L - GLA_LAG_C
        if 0 <= n < len(order):
            carry(order[n], lc.pop(n), dc.pop(n)[3], st)
    for h in range(GLA_HEADS):
        st_ref[h] = st[h]


def _gla(qk, v, la, r, g_out):
    B, L, _ = qk.shape
    nblk = L // TL
    scratch = [pltpu.VMEM((GLA_HEADS, GLA_DV, LANES), f32)]
    fwd = lambda w, cb=0: pl.BlockSpec((1, TL, w), lambda b, j: (b, j, cb))
    bwd = lambda w, cb=0: pl.BlockSpec((1, TL, w), lambda b, j: (b, nblk - 1 - j, cb))
    o_f = pl.pallas_call(
        functools.partial(_gla_kernel, reverse=False, final=False),
        grid=(B, nblk),
        in_specs=[fwd(2 * GLA_QK_W), fwd(GLA_V_W), fwd(GLA_QK_W, 0)],
        out_specs=fwd(GLA_V_W), out_shape=jax.ShapeDtypeStruct((B, L, GLA_V_W), f32),
        scratch_shapes=scratch, compiler_params=_params("parallel", "arbitrary"),
        name="gla_fwd",
    )(qk, v, la)
    return pl.pallas_call(
        functools.partial(_gla_kernel, reverse=True, final=True),
        grid=(B, nblk),
        in_specs=[bwd(2 * GLA_QK_W), bwd(GLA_V_W), bwd(GLA_QK_W, 1), bwd(GLA_V_W), bwd(GLA_V_W),
                  _const_spec((1, GLA_V_W))],
        out_specs=bwd(GLA_V_W), out_shape=jax.ShapeDtypeStruct((B, L, GLA_V_W), bf16),
        scratch_shapes=scratch, compiler_params=_params("parallel", "arbitrary"),
        name="gla_bwd",
    )(qk, v, la, o_f, r, g_out)


def _att_kernel(q_ref, kv_ref, kvp_ref, kvn_ref, o_ref, lse_ref, vt_ref, qt_ref, ot_ref, bias_ref, *, n_seq, dil):
    i = pl.program_id(0)
    R = ATT_RADIUS
    npair = ATT_HEADS // 2

    @pl.when(i == 0)
    def _():
        key = lax.broadcasted_iota(jnp.int32, (KW, 2 * TQ), 0)
        qry = lax.broadcasted_iota(jnp.int32, (KW, 2 * TQ), 1) % TQ
        odd = lax.broadcasted_iota(jnp.int32, (KW, 2 * TQ), 1) >= TQ
        rel = jnp.abs(key - R - qry)
        relf = rel.astype(f32)
        for var in range(4):
            ok = rel <= R
            if var & 1:
                ok = ok & (key >= R)
            if var & 2:
                ok = ok & (key < R + TQ)
            for p in range(npair):
                s_even, s_odd = [2.0 ** (-8.0 * (h + 1) / ATT_HEADS) * dil * LOG2E for h in (2 * p, 2 * p + 1)]
                bias_ref[var * npair + p] = jnp.where(ok, -jnp.where(odd, s_odd, s_even) * relf, NEG)

    vt_ref[...] = jnp.concatenate([kvp_ref[:, ATT_W:], kv_ref[:, ATT_W:], kvn_ref[:, ATT_W:]], axis=0).T
    qt_ref[...] = q_ref[...].T

    def kwin(a, ps):
        lo, hi = a * TQ - R, a * TQ + TQ + R
        parts = [kvp_ref[:, ps]] if lo < 0 else []
        parts.append(kv_ref[max(lo, 0):min(hi, TB), ps])
        if hi > TB:
            parts.append(kvn_ref[:, ps])
        return jnp.concatenate(parts, axis=0) if len(parts) > 1 else parts[0]

    rhalf = lax.broadcasted_iota(jnp.int32, (LANES, TQ), 0) // ATT_DH
    sub = lax.broadcasted_iota(jnp.int32, (8, TQ), 0)
    zero = jnp.zeros((), bf16)
    def tree(op, x):
        parts = [x[r * 8:(r + 1) * 8] for r in range(KW // 8)]
        while len(parts) > 1:
            parts = [op(parts[k], parts[k + 1]) for k in range(0, len(parts), 2)]
        return parts[0]

    nsub = TB // TQ
    var = []
    for a in range(nsub):
        start = i * TB + a * TQ
        var.append((jnp.where((start % n_seq) == 0, 1, 0) + jnp.where(((start + TQ) % n_seq) == 0, 2, 0)) * npair)

    def scores(a, p):
        ps = slice(p * LANES, (p + 1) * LANES)
        qt = qt_ref[ps, a * TQ:(a + 1) * TQ]
        qst = jnp.concatenate([jnp.where(rhalf == 0, qt, zero), jnp.where(rhalf == 1, qt, zero)], axis=1)
        st = _dot(kwin(a, ps), qst) + bias_ref[var[a] + p]
        m = jnp.max(tree(jnp.maximum, st), axis=0, keepdims=True)
        return st, m

    def probs(st, m):
        pe = jnp.exp2(st - m)
        l = jnp.sum(tree(jnp.add, pe), axis=0, keepdims=True)
        return pe.astype(bf16), l

    def values(a, p, pt, l, m):
        ps = slice(p * LANES, (p + 1) * LANES)
        ot = _dot(vt_ref[ps, a * TQ:a * TQ + KW], pt) * (1.0 / l)
        ot_ref[ps, a * TQ:(a + 1) * TQ] = jnp.where(rhalf == 0, ot[:, :TQ], ot[:, TQ:])
        return m + jnp.log2(l)

    items = [(a, p) for a in range(nsub) for p in range(npair)]
    sc, pr, lse = {}, {}, {}
    for t in range(len(items) + ATT_LAG_P + ATT_LAG_V):
        if t < len(items):
            sc[t] = scores(*items[t])
        if 0 <= t - ATT_LAG_P < len(items):
            pr[t - ATT_LAG_P] = probs(*sc[t - ATT_LAG_P])
        n = t - ATT_LAG_P - ATT_LAG_V
        if 0 <= n < len(items):
            lse[n] = values(*items[n], *pr.pop(n), sc.pop(n)[1])
    for a in range(nsub):
        lse8 = jnp.zeros((8, TQ), f32)
        for p in range(npair):
            row = lse[a * npair + p]
            lse8 = jnp.where(sub == 2 * p, row[:, :TQ], jnp.where(sub == 2 * p + 1, row[:, TQ:], lse8))
        lse_ref[a * TQ:(a + 1) * TQ, :] = jnp.concatenate([lse8, jnp.zeros((LANES - 8, TQ), f32)], axis=0).T
    o_ref[...] = ot_ref[...].T.astype(bf16)


def _att(q, kv, n_seq, dil):
    rows = q.shape[0]
    nhalo = rows // ATT_RADIUS
    per = TB // ATT_RADIUS
    return pl.pallas_call(
        functools.partial(_att_kernel, n_seq=n_seq, dil=dil),
        grid=(rows // TB,),
        in_specs=[pl.BlockSpec((TB, ATT_W), lambda i: (i, 0)),
                  pl.BlockSpec((TB, 2 * ATT_W), lambda i: (i, 0)),
                  pl.BlockSpec((ATT_RADIUS, 2 * ATT_W), lambda i: (jnp.maximum(i * per - 1, 0), 0)),
                  pl.BlockSpec((ATT_RADIUS, 2 * ATT_W), lambda i: (jnp.minimum((i + 1) * per, nhalo - 1), 0))],
        out_specs=[pl.BlockSpec((TB, ATT_W), lambda i: (i, 0)), pl.BlockSpec((TB, LANES), lambda i: (i, 0))],
        out_shape=[jax.ShapeDtypeStruct((rows, ATT_W), bf16), jax.ShapeDtypeStruct((rows, LANES), f32)],
        scratch_shapes=[pltpu.VMEM((ATT_W, TB + 2 * ATT_RADIUS), bf16),
                        pltpu.VMEM((ATT_W, TB), bf16),
                        pltpu.VMEM((ATT_W, TB), f32),
                        pltpu.VMEM((4 * (ATT_HEADS // 2), KW, 2 * TQ), f32)],
        compiler_params=_params("arbitrary"),
        name=f"att_d{dil}",
    )(q, kv, kv, kv)


def _mix_kernel(h_ref, og_ref, o1_ref, l1_ref, o4_ref, l4_ref, o16_ref, l16_ref, wo_ref, out_ref,
                os_ref, ls_ref, cat_ref):
    ngrp = ATT_W // LANES
    nout = D_MODEL // MXU_N
    cols = lambda c: slice(c * MXU_N, (c + 1) * MXU_N)

    og = og_ref[0]
    part = [h_ref[0, :, cols(c)] + _dot(og, wo_ref[:GLA_V_W, cols(c)]) for c in range(nout)]

    for bi, (o_ref, l_ref, d) in enumerate(((o4_ref, l4_ref, 4), (o16_ref, l16_ref, 16))):
        for c in range(d):
            ls_ref[bi, pl.ds(c, TM // d, stride=d), :] = l_ref[0, c]
            oc = o_ref[0, c].astype(f32)
            for gi in range(ngrp):
                os_ref[bi, gi, pl.ds(c, TM // d, stride=d), :] = oc[:, gi * LANES:(gi + 1) * LANES]

    l1, l4, l16 = l1_ref[0], ls_ref[0], ls_ref[1]
    m = jnp.maximum(jnp.maximum(l1, l4), l16)
    w1, w4, w16 = jnp.exp2(l1 - m), jnp.exp2(l4 - m), jnp.exp2(l16 - m)
    inv = 1.0 / (w1 + w4 + w16)
    a1, a4, a16 = w1 * inv, w4 * inv, w16 * inv
    head = lax.broadcasted_iota(jnp.int32, (TM, LANES), 1) // ATT_DH
    for gi in range(ngrp):
        idx = head + 2 * gi
        gs = slice(gi * LANES, (gi + 1) * LANES)
        o = (jnp.take_along_axis(a1, idx, axis=1) * o1_ref[0, :, gs].astype(f32)
             + jnp.take_along_axis(a4, idx, axis=1) * os_ref[0, gi]
             + jnp.take_along_axis(a16, idx, axis=1) * os_ref[1, gi])
        cat_ref[:, gs] = o.astype(bf16)
    oa = cat_ref[...]
    for c in range(nout):
        out_ref[0, :, cols(c)] = part[c] + _dot(oa, wo_ref[GLA_V_W:, cols(c)])


def _mix(h, og, o1, l1, o4, l4, o16, l16, w_out):
    B, L, D = h.shape
    tok = lambda w: pl.BlockSpec((1, TM, w), lambda b, t: (b, t, 0))
    cls = lambda d, w: pl.BlockSpec((1, d, TM // d, w), lambda b, t: (b, 0, t, 0))
    return pl.pallas_call(
        _mix_kernel,
        grid=(B, L // TM),
        in_specs=[tok(D), tok(GLA_V_W), tok(ATT_W), tok(LANES), cls(4, ATT_W), cls(4, LANES),
                  cls(16, ATT_W), cls(16, LANES), _const_spec((D, D))],
        out_specs=tok(D), out_shape=jax.ShapeDtypeStruct((B, L, D), f32),
        scratch_shapes=[pltpu.VMEM((2, ATT_W // LANES, TM, LANES), f32), pltpu.VMEM((2, TM, LANES), f32),
                        pltpu.VMEM((TM, ATT_W), bf16)],
        compiler_params=_params("parallel", "parallel"),
        name="mix",
    )(h, og, o1, l1, o4, l4, o16, l16, w_out)


def _prep_layer(g_ffn1, w1_gate, w1_up, w1_down, g_mix, w_in, w_a2_fwd, b_a_fwd, w_a2_bwd, b_a_bwd,
                g_gla_out, w_out, g_ffn2, w2_gate, w2_up, w2_down):
    c = [0]
    for w in (GLA_QK_W, GLA_QK_W, GLA_V_W, GLA_V_W, GLA_RANK, GLA_RANK, ATT_W, ATT_W, ATT_W):
        c.append(c[-1] + w)
    gq, gk, gv, gr, lrf, lrb, aq, ak, av = [w_in[:, c[i]:c[i + 1]] for i in range(9)]
    pad = jnp.zeros((D_MODEL, LR_PAD - 2 * GLA_RANK), f32)
    w_gla = jnp.concatenate([gq * GLA_DK ** -0.5, gk, gv, gr, lrf, lrb, pad], axis=1).astype(bf16)
    w_att = jnp.concatenate([aq * (ATT_DH ** -0.5 * LOG2E), ak, av], axis=1).astype(bf16)
    wa = jnp.zeros((LR_PAD, 2 * GLA_QK_W), f32)
    wa = wa.at[:GLA_RANK, :GLA_QK_W].set(w_a2_fwd).at[GLA_RANK:2 * GLA_RANK, GLA_QK_W:].set(w_a2_bwd)
    wa_hi = wa.astype(bf16)
    wa_lo = (wa - wa_hi.astype(f32)).astype(bf16)
    ba = jnp.concatenate([b_a_fwd, b_a_bwd])[None, :]
    return dict(
        ffn1=(g_ffn1[None, :], w1_gate.astype(bf16), w1_up.astype(bf16), w1_down.astype(bf16)),
        proj=(g_mix[None, :], w_gla, w_att, wa_hi, wa_lo, ba),
        g_out=g_gla_out[None, :], w_out=w_out.astype(bf16),
        ffn2=(g_ffn2[None, :], w2_gate.astype(bf16), w2_up.astype(bf16), w2_down.astype(bf16)))


def _layer(x, p, g_final):
    B, L, _ = x.shape
    h = _ffn(x, *p["ffn1"])
    qk, v, r, la, q1, kv1, q4, kv4, q16, kv16 = _proj(h, *p["proj"])
    og = _gla(qk, v, la, r, p["g_out"])
    outs = []
    for d, q, kv in ((1, q1, kv1), (4, q4, kv4), (16, q16, kv16)):
        o, lse = _att(q.reshape(B * L, ATT_W), kv.reshape(B * L, 2 * ATT_W), L // d, d)
        shape = (B, L) if d == 1 else (B, d, L // d)
        outs += [o.reshape(*shape, ATT_W), lse.reshape(*shape, LANES)]
    h = _mix(h, og, *outs, p["w_out"])
    return _ffn(h, *p["ffn2"], g_final=g_final)


def kernel(x_prompt, x_sample, g_ffn1, w1_gate, w1_up, w1_down, g_mix, w_in, w_a2_fwd, b_a_fwd, w_a2_bwd, b_a_bwd, g_gla_out, w_out, g_ffn2, w2_gate, w2_up, w2_down, g_final):
    depth = g_ffn1.shape[0]
    assert depth == 1, "final norm is fused into the last layer's second FFN"
    p = _prep_layer(g_ffn1[0], w1_gate[0], w1_up[0], w1_down[0], g_mix[0], w_in[0], w_a2_fwd[0], b_a_fwd[0],
                    w_a2_bwd[0], b_a_bwd[0], g_gla_out[0], w_out[0], g_ffn2[0], w2_gate[0], w2_up[0], w2_down[0])
    gf = g_final[None, :]
    return (_layer(x_prompt, p, gf), _layer(x_sample, p, gf))
```

```python
import functools

import jax
import jax.numpy as jnp
from jax import lax
from jax.experimental import pallas as pl
from jax.experimental.pallas import tpu as pltpu

f32 = jnp.float32
bf16 = jnp.bfloat16

D_MODEL = 1024
D_FF = 2816
GLA_HEADS = 4
GLA_DK = 64
GLA_DV = 128
GLA_RANK = 16
GLA_TAU = 16.0
GLA_CHUNK = 64
GLA_QK_W = GLA_HEADS * GLA_DK
GLA_V_W = GLA_HEADS * GLA_DV
ATT_HEADS = 8
ATT_DH = 64
ATT_W = ATT_HEADS * ATT_DH
ATT_RADIUS = 64
DILATIONS = (1, 4, 16)
EPS = 1e-6
NEG = -1e30

LOG2E = 1.4426950408889634
LN2 = 0.6931471805599453

LANES = 128
MXU_N = 256
TM = 512
FF_CHUNK = MXU_N
TL = 512
TB = 512
TQ = 128
KW = TQ + 2 * ATT_RADIUS
GLA_LAG_L = 2
GLA_LAG_C = 2
ATT_LAG_P = 2
ATT_LAG_V = 2
VMEM_LIMIT = 56 * 1024 * 1024

LR_PAD = LANES
Z_GLA_W = 2 * GLA_QK_W + 2 * GLA_V_W + LR_PAD
Z_ATT_W = 3 * ATT_W

_NT = (((1,), (1,)), ((), ()))
_TN = (((0,), (0,)), ((), ()))


def _dot(a, b):
    return jnp.dot(a, b, preferred_element_type=f32)


def _rms(x, g):
    return x * lax.rsqrt(jnp.mean(x * x, axis=-1, keepdims=True) + EPS) * g


def _split(x):
    hi = x.astype(bf16)
    lo = (x - hi.astype(f32)).astype(bf16)
    return hi, lo


def _const_spec(shape):
    zeros = (0,) * len(shape)
    return pl.BlockSpec(shape, lambda *_: zeros, pipeline_mode=pl.Buffered(1))


def _params(*sem):
    return pltpu.CompilerParams(dimension_semantics=sem, vmem_limit_bytes=VMEM_LIMIT)


def _swiglu_residual(x, g_ref, wg_ref, wu_ref, wd_ref):
    xn = _rms(x, g_ref[...]).astype(bf16)
    acc = jnp.zeros(x.shape, f32)
    for c in range(D_FF // FF_CHUNK):
        sl = slice(c * FF_CHUNK, (c + 1) * FF_CHUNK)
        g = _dot(xn, wg_ref[:, sl])
        u = _dot(xn, wu_ref[:, sl])
        a = (g * (1.0 / (1.0 + jnp.exp(-g))) * u).astype(bf16)
        acc = acc + _dot(a, wd_ref[sl, :])
    return x + 0.5 * acc


def _ffn_kernel(x_ref, g_ref, wg_ref, wu_ref, wd_ref, o_ref):
    o_ref[0] = _swiglu_residual(x_ref[0], g_ref, wg_ref, wu_ref, wd_ref)


def _ffn_specs(D):
    return [_const_spec((1, D)), _const_spec((D, D_FF)), _const_spec((D, D_FF)), _const_spec((D_FF, D))]


def _ffn(x, g, wg, wu, wd):
    B, L, D = x.shape
    tok = pl.BlockSpec((1, TM, D), lambda b, t: (b, t, 0))
    return pl.pallas_call(
        _ffn_kernel,
        grid=(B, L // TM), in_specs=[tok] + _ffn_specs(D), out_specs=tok,
        out_shape=jax.ShapeDtypeStruct((B, L, D), f32),
        compiler_params=_params("parallel", "parallel"),
        name="ffn",
    )(x, g, wg, wu, wd)


def _proj_kernel(h_ref, g_ref, wg_ref, wa_ref, whi_ref, wlo_ref, ba_ref,
                 qk_ref, v_ref, r_ref, la_ref, q1_ref, kv1_ref, q4_ref, kv4_ref, q16_ref, kv16_ref,
                 zs_ref, z4_ref):
    un = _rms(h_ref[0], g_ref[...]).astype(bf16)
    ngrp = Z_ATT_W // LANES
    per = MXU_N // LANES
    n4 = TM // 4

    def att_cols(c):
        z = _dot(un, wa_ref[:, c * MXU_N:(c + 1) * MXU_N])
        if c * MXU_N < ATT_W:
            q1_ref[0, :, c * MXU_N:(c + 1) * MXU_N] = z.astype(bf16)
        else:
            kv1_ref[0, :, c * MXU_N - ATT_W:(c + 1) * MXU_N - ATT_W] = z.astype(bf16)
        for k in range(per):
            zs_ref[c * per + k] = z[:, k * LANES:(k + 1) * LANES]

    def put(q_ref, kv_ref, c, gi, rows):
        col = gi * LANES
        if col < ATT_W:
            q_ref[0, c, :, col:col + LANES] = rows.astype(bf16)
        else:
            kv_ref[0, c, :, col - ATT_W:col - ATT_W + LANES] = rows.astype(bf16)

    def permute4(classes):
        for c in classes:
            for gi in range(ngrp):
                rows = zs_ref[gi, pl.ds(c, n4, stride=4), :]
                z4_ref[gi, c * n4:(c + 1) * n4, :] = rows
                put(q4_ref, kv4_ref, c, gi, rows)

    def permute16(classes):
        for c4 in classes:
            for m in range(4):
                for gi in range(ngrp):
                    put(q16_ref, kv16_ref, c4 + 4 * m, gi, z4_ref[gi, pl.ds(c4 * n4 + m, n4 // 4, stride=4), :])

    def gla_cols(c):
        ref, off = ((qk_ref, 0), (qk_ref, MXU_N), (v_ref, 0), (v_ref, MXU_N), (r_ref, 0), (r_ref, MXU_N))[c]
        ref[0, :, off:off + MXU_N] = _dot(un, wg_ref[:, c * MXU_N:(c + 1) * MXU_N]).astype(bf16)

    lr_hi, lr_lo = _split(_dot(un, wg_ref[:, Z_GLA_W - LR_PAD:]))
    att_cols(0)
    pre = _dot(lr_hi, whi_ref[...]) + _dot(lr_hi, wlo_ref[...]) + _dot(lr_lo, whi_ref[...]) + ba_ref[...]
    att_cols(1)
    att_cols(2)
    la_ref[0] = (jnp.minimum(pre, 0.0) - jnp.log(1.0 + jnp.exp(-jnp.abs(pre)))) * (1.0 / GLA_TAU)
    for c in range(3, Z_ATT_W // MXU_N):
        att_cols(c)
    gla_cols(0)
    permute4(range(0, 2))
    gla_cols(1)
    permute4(range(2, 4))
    for c in range(2, 6):
        gla_cols(c)
        permute16(range(c - 2, c - 1))


def _proj(h, g, w_gla, w_att, wa_hi, wa_lo, ba):
    B, L, D = h.shape
    tok = lambda w: pl.BlockSpec((1, TM, w), lambda b, t: (b, t, 0))
    cls = lambda d, w: pl.BlockSpec((1, d, TM // d, w), lambda b, t: (b, 0, t, 0))
    sds = jax.ShapeDtypeStruct
    out_shape = [sds((B, L, 2 * GLA_QK_W), bf16), sds((B, L, GLA_V_W), bf16), sds((B, L, GLA_V_W), bf16),
                 sds((B, L, 2 * GLA_QK_W), f32),
                 sds((B, L, ATT_W), bf16), sds((B, L, 2 * ATT_W), bf16)]
    out_specs = [tok(2 * GLA_QK_W), tok(GLA_V_W), tok(GLA_V_W), tok(2 * GLA_QK_W), tok(ATT_W), tok(2 * ATT_W)]
    for d in DILATIONS[1:]:
        out_shape += [sds((B, d, L // d, ATT_W), bf16), sds((B, d, L // d, 2 * ATT_W), bf16)]
        out_specs += [cls(d, ATT_W), cls(d, 2 * ATT_W)]
    return pl.pallas_call(
        _proj_kernel,
        grid=(B, L // TM),
        in_specs=[tok(D), _const_spec((1, D)), _const_spec((D, Z_GLA_W)), _const_spec((D, Z_ATT_W)),
                  _const_spec((LR_PAD, 2 * GLA_QK_W)), _const_spec((LR_PAD, 2 * GLA_QK_W)),
                  _const_spec((1, 2 * GLA_QK_W))],
        out_specs=out_specs, out_shape=out_shape,
        scratch_shapes=[pltpu.VMEM((Z_ATT_W // LANES, TM, LANES), f32)] * 2,
        compiler_params=_params("parallel", "parallel"),
        name="proj",
    )(h, g, w_gla, w_att, wa_hi, wa_lo, ba)


def _gla_kernel(qk_ref, v_ref, la_ref, *rest, reverse, final):
    if final:
        of_ref, r_ref, go_ref, o_ref, st_ref = rest
    else:
        o_ref, st_ref = rest
    C = GLA_CHUNK

    @pl.when(pl.program_id(1) == 0)
    def _():
        st_ref[...] = jnp.zeros(st_ref.shape, f32)

    row = lax.broadcasted_iota(jnp.int32, (C, C), 0)
    col = lax.broadcasted_iota(jnp.int32, (C, C), 1)
    tri = (col >= row) if reverse else (col <= row)
    tri_b = tri.astype(bf16)
    half = lax.broadcasted_iota(jnp.int32, (C, LANES), 1) // GLA_DK
    end = 0 if reverse else C - 1
    order = list(range(TL // C - 1, -1, -1) if reverse else range(TL // C))
    pair = lambda h: slice((h // 2) * LANES, (h // 2 + 1) * LANES)

    def decays(ci):
        rs = slice(ci * C, (ci + 1) * C)
        la_hi, la_lo = _split(la_ref[0, rs, :])
        b = _dot(tri_b, la_hi) + _dot(tri_b, la_lo)
        b_end = b[end:end + 1, :]
        qk = qk_ref[0, rs, :].astype(f32)
        q, k = qk[:, :GLA_QK_W], qk[:, GLA_QK_W:]
        q_dec = (q * jnp.exp(b)).astype(bf16)
        k_inv = (k * jnp.exp(-b)).astype(bf16)
        k_end = (k * jnp.exp(b_end - b)).astype(bf16)
        return q_dec, k_inv, k_end, jnp.exp(b_end)

    def local(ci, q_dec, k_inv, k_end, dec):
        rs = slice(ci * C, (ci + 1) * C)
        out = []
        for h in range(GLA_HEADS):
            qd = jnp.where(half == (h % 2), q_dec[:, pair(h)], jnp.zeros((), bf16))
            a = lax.dot_general(qd, k_inv[:, pair(h)], _NT, preferred_element_type=f32)
            a = jnp.where(tri, a, 0.0).astype(bf16)
            vh = v_ref[0, rs, h * GLA_DV:(h + 1) * GLA_DV]
            kv = lax.dot_general(vh, k_end[:, pair(h)], _TN, preferred_element_type=f32)
            out.append((qd, _dot(a, vh), kv))
        return out

    def carry(ci, heads, dec, st):
        rs = slice(ci * C, (ci + 1) * C)
        for h, (qd, o, kv) in enumerate(heads):
            o = o + lax.dot_general(qd, st[h].astype(bf16), _NT, preferred_element_type=f32)
            st[h] = st[h] * dec[:, pair(h)] + kv
            vs = slice(h * GLA_DV, (h + 1) * GLA_DV)
            if final:
                o = o + of_ref[0, rs, vs]
                o = o * lax.rsqrt(jnp.mean(o * o, axis=-1, keepdims=True) + EPS) * go_ref[:, vs]
                r = r_ref[0, rs, vs].astype(f32)
                o_ref[0, rs, vs] = (o * (r * (1.0 / (1.0 + jnp.exp(-r))))).astype(bf16)
            else:
                o_ref[0, rs, vs] = o

    st = [st_ref[h] for h in range(GLA_HEADS)]
    dc, lc = {}, {}
    for t in range(len(order) + GLA_LAG_L + GLA_LAG_C):
        if t < len(order):
            dc[t] = decays(order[t])
        n = t - GLA_LAG_L
        if 0 <= n < len(order):
            lc[n] = local(order[n], *dc[n])
        n = t - GLA_LAG_L - GLA_LAG_C
        if 0 <= n < len(order):
            carry(order[n], lc.pop(n), dc.pop(n)[3], st)
    for h in range(GLA_HEADS):
        st_ref[h] = st[h]


def _gla(qk, v, la, r, g_out):
    B, L, _ = qk.shape
    nblk = L // TL
    scratch = [pltpu.VMEM((GLA_HEADS, GLA_DV, LANES), f32)]
    fwd = lambda w, cb=0: pl.BlockSpec((1, TL, w), lambda b, j: (b, j, cb))
    bwd = lambda w, cb=0: pl.BlockSpec((1, TL, w), lambda b, j: (b, nblk - 1 - j, cb))
    o_f = pl.pallas_call(
        functools.partial(_gla_kernel, reverse=False, final=False),
        grid=(B, nblk),
        in_specs=[fwd(2 * GLA_QK_W), fwd(GLA_V_W), fwd(GLA_QK_W, 0)],
        out_specs=fwd(GLA_V_W), out_shape=jax.ShapeDtypeStruct((B, L, GLA_V_W), f32),
        scratch_shapes=scratch, compiler_params=_params("parallel", "arbitrary"),
        name="gla_fwd",
    )(qk, v, la)
    return pl.pallas_call(
        functools.partial(_gla_kernel, reverse=True, final=True),
        grid=(B, nblk),
        in_specs=[bwd(2 * GLA_QK_W), bwd(GLA_V_W), bwd(GLA_QK_W, 1), bwd(GLA_V_W), bwd(GLA_V_W),
                  _const_spec((1, GLA_V_W))],
        out_specs=bwd(GLA_V_W), out_shape=jax.ShapeDtypeStruct((B, L, GLA_V_W), bf16),
        scratch_shapes=scratch, compiler_params=_params("parallel", "arbitrary"),
        name="gla_bwd",
    )(qk, v, la, o_f, r, g_out)


def _att_kernel(q_ref, kv_ref, kvp_ref, kvn_ref, o_ref, lse_ref, vt_ref, qt_ref, ot_ref, bias_ref, *, n_seq, dil):
    i = pl.program_id(0)
    R = ATT_RADIUS
    npair = ATT_HEADS // 2

    @pl.when(i == 0)
    def _():
        key = lax.broadcasted_iota(jnp.int32, (KW, 2 * TQ), 0)
        qry = lax.broadcasted_iota(jnp.int32, (KW, 2 * TQ), 1) % TQ
        odd = lax.broadcasted_iota(jnp.int32, (KW, 2 * TQ), 1) >= TQ
        rel = jnp.abs(key - R - qry)
        relf = rel.astype(f32)
        for var in range(4):
            ok = rel <= R
            if var & 1:
                ok = ok & (key >= R)
            if var & 2:
                ok = ok & (key < R + TQ)
            for p in range(npair):
                s_even, s_odd = [2.0 ** (-8.0 * (h + 1) / ATT_HEADS) * dil * LOG2E for h in (2 * p, 2 * p + 1)]
                bias_ref[var * npair + p] = jnp.where(ok, -jnp.where(odd, s_odd, s_even) * relf, NEG)

    vt_ref[...] = jnp.concatenate([kvp_ref[:, ATT_W:], kv_ref[:, ATT_W:], kvn_ref[:, ATT_W:]], axis=0).T
    qt_ref[...] = q_ref[...].T

    def kwin(a, ps):
        lo, hi = a * TQ - R, a * TQ + TQ + R
        parts = [kvp_ref[:, ps]] if lo < 0 else []
        parts.append(kv_ref[max(lo, 0):min(hi, TB), ps])
        if hi > TB:
            parts.append(kvn_ref[:, ps])
        return jnp.concatenate(parts, axis=0) if len(parts) > 1 else parts[0]

    rhalf = lax.broadcasted_iota(jnp.int32, (LANES, TQ), 0) // ATT_DH
    sub = lax.broadcasted_iota(jnp.int32, (8, TQ), 0)
    zero = jnp.zeros((), bf16)
    def tree(op, x):
        parts = [x[r * 8:(r + 1) * 8] for r in range(KW // 8)]
        while len(parts) > 1:
            parts = [op(parts[k], parts[k + 1]) for k in range(0, len(parts), 2)]
        return parts[0]

    nsub = TB // TQ
    var = []
    for a in range(nsub):
        start = i * TB + a * TQ
        var.append((jnp.where((start % n_seq) == 0, 1, 0) + jnp.where(((start + TQ) % n_seq) == 0, 2, 0)) * npair)

    def scores(a, p):
        ps = slice(p * LANES, (p + 1) * LANES)
        qt = qt_ref[ps, a * TQ:(a + 1) * TQ]
        qst = jnp.concatenate([jnp.where(rhalf == 0, qt, zero), jnp.where(rhalf == 1, qt, zero)], axis=1)
        st = _dot(kwin(a, ps), qst) + bias_ref[var[a] + p]
        m = jnp.max(tree(jnp.maximum, st), axis=0, keepdims=True)
        return st, m

    def probs(st, m):
        pe = jnp.exp2(st - m)
        l = jnp.sum(tree(jnp.add, pe), axis=0, keepdims=True)
        return pe.astype(bf16), l

    def values(a, p, pt, l, m):
        ps = slice(p * LANES, (p + 1) * LANES)
        ot = _dot(vt_ref[ps, a * TQ:a * TQ + KW], pt) * (1.0 / l)
        ot_ref[ps, a * TQ:(a + 1) * TQ] = jnp.where(rhalf == 0, ot[:, :TQ], ot[:, TQ:])
        return m + jnp.log2(l)

    items = [(a, p) for a in range(nsub) for p in range(npair)]
    sc, pr, lse = {}, {}, {}
    for t in range(len(items) + ATT_LAG_P + ATT_LAG_V):
        if t < len(items):
            sc[t] = scores(*items[t])
        if 0 <= t - ATT_LAG_P < len(items):
            pr[t - ATT_LAG_P] = probs(*sc[t - ATT_LAG_P])
        n = t - ATT_LAG_P - ATT_LAG_V
        if 0 <= n < len(items):
            lse[n] = values(*items[n], *pr.pop(n), sc.pop(n)[1])
    for a in range(nsub):
        lse8 = jnp.zeros((8, TQ), f32)
        for p in range(npair):
            row = lse[a * npair + p]
            lse8 = jnp.where(sub == 2 * p, row[:, :TQ], jnp.where(sub == 2 * p + 1, row[:, TQ:], lse8))
        lse_ref[a * TQ:(a + 1) * TQ, :] = jnp.concatenate([lse8, jnp.zeros((LANES - 8, TQ), f32)], axis=0).T
    o_ref[...] = ot_ref[...].T.astype(bf16)


def _att(q, kv, n_seq, dil):
    rows = q.shape[0]
    nhalo = rows // ATT_RADIUS
    per = TB // ATT_RADIUS
    return pl.pallas_call(
        functools.partial(_att_kernel, n_seq=n_seq, dil=dil),
        grid=(rows // TB,),
        in_specs=[pl.BlockSpec((TB, ATT_W), lambda i: (i, 0)),
                  pl.BlockSpec((TB, 2 * ATT_W), lambda i: (i, 0)),
                  pl.BlockSpec((ATT_RADIUS, 2 * ATT_W), lambda i: (jnp.maximum(i * per - 1, 0), 0)),
                  pl.BlockSpec((ATT_RADIUS, 2 * ATT_W), lambda i: (jnp.minimum((i + 1) * per, nhalo - 1), 0))],
        out_specs=[pl.BlockSpec((TB, ATT_W), lambda i: (i, 0)), pl.BlockSpec((TB, LANES), lambda i: (i, 0))],
        out_shape=[jax.ShapeDtypeStruct((rows, ATT_W), bf16), jax.ShapeDtypeStruct((rows, LANES), f32)],
        scratch_shapes=[pltpu.VMEM((ATT_W, TB + 2 * ATT_RADIUS), bf16),
                        pltpu.VMEM((ATT_W, TB), bf16),
                        pltpu.VMEM((ATT_W, TB), f32),
                        pltpu.VMEM((4 * (ATT_HEADS // 2), KW, 2 * TQ), f32)],
        compiler_params=_params("arbitrary"),
        name=f"att_d{dil}",
    )(q, kv, kv, kv)


def _mix_ffn_kernel(h_ref, og_ref, o1_ref, l1_ref, o4_ref, l4_ref, o16_ref, l16_ref, wo_ref,
                    g_ref, wg_ref, wu_ref, wd_ref, gf_ref, out_ref, os_ref, ls_ref, cat_ref):
    ngrp = ATT_W // LANES
    nout = D_MODEL // MXU_N
    cols = lambda c: slice(c * MXU_N, (c + 1) * MXU_N)
    n4 = TM // 4

    og = og_ref[0]
    part = [h_ref[0, :, cols(c)] + _dot(og, wo_ref[:GLA_V_W, cols(c)]) for c in range(nout)]

    for c in range(16):
        c4, m = c % 4, c // 4
        ls_ref[2, pl.ds(c4 * n4 + m, n4 // 4, stride=4), :] = l16_ref[0, c]
        oc = o16_ref[0, c].astype(f32)
        for gi in range(ngrp):
            os_ref[2, gi, pl.ds(c4 * n4 + m, n4 // 4, stride=4), :] = oc[:, gi * LANES:(gi + 1) * LANES]
    for c in range(4):
        ls_ref[0, pl.ds(c, n4, stride=4), :] = l4_ref[0, c]
        ls_ref[1, pl.ds(c, n4, stride=4), :] = ls_ref[2, c * n4:(c + 1) * n4, :]
        oc = o4_ref[0, c].astype(f32)
        for gi in range(ngrp):
            os_ref[0, gi, pl.ds(c, n4, stride=4), :] = oc[:, gi * LANES:(gi + 1) * LANES]
            os_ref[1, gi, pl.ds(c, n4, stride=4), :] = os_ref[2, gi, c * n4:(c + 1) * n4, :]

    l1, l4, l16 = l1_ref[0], ls_ref[0], ls_ref[1]
    m = jnp.maximum(jnp.maximum(l1, l4), l16)
    w1, w4, w16 = jnp.exp2(l1 - m), jnp.exp2(l4 - m), jnp.exp2(l16 - m)
    inv = 1.0 / (w1 + w4 + w16)
    a1, a4, a16 = w1 * inv, w4 * inv, w16 * inv
    head = lax.broadcasted_iota(jnp.int32, (TM, LANES), 1) // ATT_DH
    for gi in range(ngrp):
        idx = head + 2 * gi
        gs = slice(gi * LANES, (gi + 1) * LANES)
        o = (jnp.take_along_axis(a1, idx, axis=1) * o1_ref[0, :, gs].astype(f32)
             + jnp.take_along_axis(a4, idx, axis=1) * os_ref[0, gi]
             + jnp.take_along_axis(a16, idx, axis=1) * os_ref[1, gi])
        cat_ref[:, gs] = o.astype(bf16)
    oa = cat_ref[...]
    h2 = jnp.concatenate([part[c] + _dot(oa, wo_ref[GLA_V_W:, cols(c)]) for c in range(nout)], axis=-1)
    out_ref[0] = _rms(_swiglu_residual(h2, g_ref, wg_ref, wu_ref, wd_ref), gf_ref[...])


def _mix_ffn(h, og, o1, l1, o4, l4, o16, l16, w_out, g, wg, wu, wd, g_final):
    B, L, D = h.shape
    tok = lambda w: pl.BlockSpec((1, TM, w), lambda b, t: (b, t, 0))
    cls = lambda d, w: pl.BlockSpec((1, d, TM // d, w), lambda b, t: (b, 0, t, 0))
    return pl.pallas_call(
        _mix_ffn_kernel,
        grid=(B, L // TM),
        in_specs=[tok(D), tok(GLA_V_W), tok(ATT_W), tok(LANES), cls(4, ATT_W), cls(4, LANES),
                  cls(16, ATT_W), cls(16, LANES), _const_spec((D, D))] + _ffn_specs(D) + [_const_spec((1, D))],
        out_specs=tok(D), out_shape=jax.ShapeDtypeStruct((B, L, D), f32),
        scratch_shapes=[pltpu.VMEM((3, ATT_W // LANES, TM, LANES), f32), pltpu.VMEM((3, TM, LANES), f32),
                        pltpu.VMEM((TM, ATT_W), bf16)],
        compiler_params=_params("parallel", "parallel"),
        name="mix_ffn",
    )(h, og, o1, l1, o4, l4, o16, l16, w_out, g, wg, wu, wd, g_final)


def _prep_layer(g_ffn1, w1_gate, w1_up, w1_down, g_mix, w_in, w_a2_fwd, b_a_fwd, w_a2_bwd, b_a_bwd,
                g_gla_out, w_out, g_ffn2, w2_gate, w2_up, w2_down):
    c = [0]
    for w in (GLA_QK_W, GLA_QK_W, GLA_V_W, GLA_V_W, GLA_RANK, GLA_RANK, ATT_W, ATT_W, ATT_W):
        c.append(c[-1] + w)
    gq, gk, gv, gr, lrf, lrb, aq, ak, av = [w_in[:, c[i]:c[i + 1]] for i in range(9)]
    pad = jnp.zeros((D_MODEL, LR_PAD - 2 * GLA_RANK), f32)
    w_gla = jnp.concatenate([gq * GLA_DK ** -0.5, gk, gv, gr, lrf, lrb, pad], axis=1).astype(bf16)
    w_att = jnp.concatenate([aq * (ATT_DH ** -0.5 * LOG2E), ak, av], axis=1).astype(bf16)
    wa = jnp.zeros((LR_PAD, 2 * GLA_QK_W), f32)
    wa = wa.at[:GLA_RANK, :GLA_QK_W].set(w_a2_fwd).at[GLA_RANK:2 * GLA_RANK, GLA_QK_W:].set(w_a2_bwd)
    wa_hi = wa.astype(bf16)
    wa_lo = (wa - wa_hi.astype(f32)).astype(bf16)
    ba = jnp.concatenate([b_a_fwd, b_a_bwd])[None, :]
    return dict(
        ffn1=(g_ffn1[None, :], w1_gate.astype(bf16), w1_up.astype(bf16), w1_down.astype(bf16)),
        proj=(g_mix[None, :], w_gla, w_att, wa_hi, wa_lo, ba),
        g_out=g_gla_out[None, :], w_out=w_out.astype(bf16),
        ffn2=(g_ffn2[None, :], w2_gate.astype(bf16), w2_up.astype(bf16), w2_down.astype(bf16)))


def _layer(x, p, g_final):
    B, L, _ = x.shape
    h = _ffn(x, *p["ffn1"])
    qk, v, r, la, q1, kv1, q4, kv4, q16, kv16 = _proj(h, *p["proj"])
    og = _gla(qk, v, la, r, p["g_out"])
    outs = []
    for d, q, kv in ((1, q1, kv1), (4, q4, kv4), (16, q16, kv16)):
        o, lse = _att(q.reshape(B * L, ATT_W), kv.reshape(B * L, 2 * ATT_W), L // d, d)
        shape = (B, L) if d == 1 else (B, d, L // d)
        outs += [o.reshape(*shape, ATT_W), lse.reshape(*shape, LANES)]
    return _mix_ffn(h, og, *outs, p["w_out"], *p["ffn2"], g_final)


def kernel(x_prompt, x_sample, g_ffn1, w1_gate, w1_up, w1_down, g_mix, w_in, w_a2_fwd, b_a_fwd, w_a2_bwd, b_a_bwd, g_gla_out, w_out, g_ffn2, w2_gate, w2_up, w2_down, g_final):
    depth = g_ffn1.shape[0]
    assert depth == 1, "final norm is fused into the last layer's second FFN"
    p = _prep_layer(g_ffn1[0], w1_gate[0], w1_up[0], w1_down[0], g_mix[0], w_in[0], w_a2_fwd[0], b_a_fwd[0],
                    w_a2_bwd[0], b_a_bwd[0], g_gla_out[0], w_out[0], g_ffn2[0], w2_gate[0], w2_up[0], w2_down[0])
    gf = g_final[None, :]
    return (_layer(x_prompt, p, gf), _layer(x_sample, p, gf))
```

```python
import functools

import jax
import jax.numpy as jnp
from jax import lax
from jax.experimental import pallas as pl
from jax.experimental.pallas import tpu as pltpu

f32 = jnp.float32
bf16 = jnp.bfloat16

D_MODEL = 1024
D_FF = 2816
GLA_HEADS = 4
GLA_DK = 64
GLA_DV = 128
GLA_RANK = 16
GLA_TAU = 16.0
GLA_CHUNK = 64
GLA_QK_W = GLA_HEADS * GLA_DK
GLA_V_W = GLA_HEADS * GLA_DV
ATT_HEADS = 8
ATT_DH = 64
ATT_W = ATT_HEADS * ATT_DH
ATT_RADIUS = 64
DILATIONS = (1, 4, 16)
EPS = 1e-6
NEG = -1e30

LOG2E = 1.4426950408889634
LN2 = 0.6931471805599453

LANES = 128
MXU_N = 256
TM = 512
TM_FFN = 1024
FF_CHUNK = MXU_N
TL_FWD = 1024
TL_BWD = 512
TB = 2048
TQ = 128
KW = TQ + 2 * ATT_RADIUS
GLA_LAG_L = 2
GLA_LAG_C = 2
GLA_LAG_E = 2
ATT_LAG_P = 2
ATT_LAG_V = 2
VMEM_LIMIT = 56 * 1024 * 1024

LR_PAD = LANES
Z_GLA_W = 2 * GLA_QK_W + 2 * GLA_V_W + LR_PAD
Z_ATT_W = 3 * ATT_W

_NT = (((1,), (1,)), ((), ()))
_TN = (((0,), (0,)), ((), ()))


def _dot(a, b):
    return jnp.dot(a, b, preferred_element_type=f32)


def _rms(x, g):
    return x * lax.rsqrt(jnp.mean(x * x, axis=-1, keepdims=True) + EPS) * g


def _split(x):
    hi = x.astype(bf16)
    lo = (x - hi.astype(f32)).astype(bf16)
    return hi, lo


def _const_spec(shape):
    zeros = (0,) * len(shape)
    return pl.BlockSpec(shape, lambda *_: zeros, pipeline_mode=pl.Buffered(1))


def _params(*sem):
    return pltpu.CompilerParams(dimension_semantics=sem, vmem_limit_bytes=VMEM_LIMIT)


def _swiglu_residual(x, g_ref, wg_ref, wu_ref, wd_ref):
    xn = _rms(x, g_ref[...]).astype(bf16)
    acc = jnp.zeros(x.shape, f32)
    for c in range(D_FF // FF_CHUNK):
        sl = slice(c * FF_CHUNK, (c + 1) * FF_CHUNK)
        g = _dot(xn, wg_ref[:, sl])
        u = _dot(xn, wu_ref[:, sl])
        a = (g * (1.0 / (1.0 + jnp.exp(-g))) * u).astype(bf16)
        acc = acc + _dot(a, wd_ref[sl, :])
    return x + 0.5 * acc


def _ffn_kernel(x_ref, g_ref, wg_ref, wu_ref, wd_ref, o_ref):
    o_ref[0] = _swiglu_residual(x_ref[0], g_ref, wg_ref, wu_ref, wd_ref)


def _ffn_specs(D):
    return [_const_spec((1, D)), _const_spec((D, D_FF)), _const_spec((D, D_FF)), _const_spec((D_FF, D))]


def _ffn(x, g, wg, wu, wd):
    B, L, D = x.shape
    tok = pl.BlockSpec((1, TM_FFN, D), lambda b, t: (b, t, 0))
    return pl.pallas_call(
        _ffn_kernel,
        grid=(B, L // TM_FFN), in_specs=[tok] + _ffn_specs(D), out_specs=tok,
        out_shape=jax.ShapeDtypeStruct((B, L, D), f32),
        compiler_params=_params("parallel", "parallel"),
        name="ffn",
    )(x, g, wg, wu, wd)


def _proj_kernel(h_ref, g_ref, wg_ref, wa_ref, whi_ref, wlo_ref, ba_ref, go_ref,
                 qk_ref, v_ref, r_ref, la_ref, q1_ref, kv1_ref, q4_ref, kv4_ref, q16_ref, kv16_ref,
                 zs_ref, z4_ref):
    un = _rms(h_ref[0], g_ref[...]).astype(bf16)
    ngrp = Z_ATT_W // LANES
    per = MXU_N // LANES
    n4 = TM // 4

    def att_cols(c):
        z = _dot(un, wa_ref[:, c * MXU_N:(c + 1) * MXU_N])
        if c * MXU_N < ATT_W:
            q1_ref[0, :, c * MXU_N:(c + 1) * MXU_N] = z.astype(bf16)
        else:
            kv1_ref[0, :, c * MXU_N - ATT_W:(c + 1) * MXU_N - ATT_W] = z.astype(bf16)
        for k in range(per):
            zs_ref[c * per + k] = z[:, k * LANES:(k + 1) * LANES]

    def put(q_ref, kv_ref, c, gi, rows):
        col = gi * LANES
        if col < ATT_W:
            q_ref[0, c, :, col:col + LANES] = rows.astype(bf16)
        else:
            kv_ref[0, c, :, col - ATT_W:col - ATT_W + LANES] = rows.astype(bf16)

    def permute4(classes):
        for c in classes:
            for gi in range(ngrp):
                rows = zs_ref[gi, pl.ds(c, n4, stride=4), :]
                z4_ref[gi, c * n4:(c + 1) * n4, :] = rows
                put(q4_ref, kv4_ref, c, gi, rows)

    def permute16(classes):
        for c4 in classes:
            for m in range(4):
                for gi in range(ngrp):
                    put(q16_ref, kv16_ref, c4 + 4 * m, gi, z4_ref[gi, pl.ds(c4 * n4 + m, n4 // 4, stride=4), :])

    def gla_cols(c):
        ref, off = ((qk_ref, 0), (qk_ref, MXU_N), (v_ref, 0), (v_ref, MXU_N), (r_ref, 0), (r_ref, MXU_N))[c]
        z = _dot(un, wg_ref[:, c * MXU_N:(c + 1) * MXU_N])
        if ref is r_ref:
            z = z * (1.0 / (1.0 + jnp.exp(-z))) * go_ref[:, off:off + MXU_N]
        ref[0, :, off:off + MXU_N] = z.astype(bf16)

    lr_hi, lr_lo = _split(_dot(un, wg_ref[:, Z_GLA_W - LR_PAD:]))
    att_cols(0)
    pre = _dot(lr_hi, whi_ref[...]) + _dot(lr_hi, wlo_ref[...]) + _dot(lr_lo, whi_ref[...]) + ba_ref[...]
    att_cols(1)
    att_cols(2)
    la_ref[0] = (jnp.minimum(pre, 0.0) - jnp.log(1.0 + jnp.exp(-jnp.abs(pre)))) * (1.0 / GLA_TAU)
    for c in range(3, Z_ATT_W // MXU_N):
        att_cols(c)
    gla_cols(0)
    permute4(range(0, 2))
    gla_cols(1)
    permute4(range(2, 4))
    for c in range(2, 6):
        gla_cols(c)
        permute16(range(c - 2, c - 1))


def _proj(h, g, w_gla, w_att, wa_hi, wa_lo, ba, g_out):
    B, L, D = h.shape
    tok = lambda w: pl.BlockSpec((1, TM, w), lambda b, t: (b, t, 0))
    cls = lambda d, w: pl.BlockSpec((1, d, TM // d, w), lambda b, t: (b, 0, t, 0))
    sds = jax.ShapeDtypeStruct
    out_shape = [sds((B, L, 2 * GLA_QK_W), bf16), sds((B, L, GLA_V_W), bf16), sds((B, L, GLA_V_W), bf16),
                 sds((B, L, 2 * GLA_QK_W), f32),
                 sds((B, L, ATT_W), bf16), sds((B, L, 2 * ATT_W), bf16)]
    out_specs = [tok(2 * GLA_QK_W), tok(GLA_V_W), tok(GLA_V_W), tok(2 * GLA_QK_W), tok(ATT_W), tok(2 * ATT_W)]
    for d in DILATIONS[1:]:
        out_shape += [sds((B, d, L // d, ATT_W), bf16), sds((B, d, L // d, 2 * ATT_W), bf16)]
        out_specs += [cls(d, ATT_W), cls(d, 2 * ATT_W)]
    return pl.pallas_call(
        _proj_kernel,
        grid=(B, L // TM),
        in_specs=[tok(D), _const_spec((1, D)), _const_spec((D, Z_GLA_W)), _const_spec((D, Z_ATT_W)),
                  _const_spec((LR_PAD, 2 * GLA_QK_W)), _const_spec((LR_PAD, 2 * GLA_QK_W)),
                  _const_spec((1, 2 * GLA_QK_W)), _const_spec((1, GLA_V_W))],
        out_specs=out_specs, out_shape=out_shape,
        scratch_shapes=[pltpu.VMEM((Z_ATT_W // LANES, TM, LANES), f32)] * 2,
        compiler_params=_params("parallel", "parallel"),
        name="proj",
    )(h, g, w_gla, w_att, wa_hi, wa_lo, ba, g_out)


def _gla_kernel(qk_ref, v_ref, la_ref, *rest, reverse, final):
    if final:
        of_ref, gate_ref, o_ref, st_ref = rest
    else:
        o_ref, st_ref = rest
    C = GLA_CHUNK
    npair = GLA_HEADS // 2
    zero = jnp.zeros((), bf16)

    @pl.when(pl.program_id(1) == 0)
    def _():
        st_ref[...] = jnp.zeros(st_ref.shape, f32)

    row = lax.broadcasted_iota(jnp.int32, (C, C), 0)
    col = lax.broadcasted_iota(jnp.int32, (C, C), 1)
    tri = (col >= row) if reverse else (col <= row)
    tri_b = tri.astype(bf16)
    row2 = lax.broadcasted_iota(jnp.int32, (C, 2 * C), 0)
    col2 = lax.broadcasted_iota(jnp.int32, (C, 2 * C), 1) % C
    tri2 = (col2 >= row2) if reverse else (col2 <= row2)
    half = lax.broadcasted_iota(jnp.int32, (C, LANES), 1) // GLA_DK
    diag = (lax.broadcasted_iota(jnp.int32, (LANES, 2 * GLA_DV), 0) // GLA_DK
            == lax.broadcasted_iota(jnp.int32, (LANES, 2 * GLA_DV), 1) // GLA_DV)
    zpad = jnp.zeros((LANES - C, LANES), f32)
    zv = jnp.zeros((C, GLA_DV), bf16)
    end = 0 if reverse else C - 1
    nchunk = qk_ref.shape[1] // C
    order = list(range(nchunk - 1, -1, -1) if reverse else range(nchunk))
    lanes = lambda p: slice(p * LANES, (p + 1) * LANES)

    def decays(ci):
        rs = slice(ci * C, (ci + 1) * C)
        la_hi, la_lo = _split(la_ref[0, rs, :])
        b = _dot(tri_b, la_hi) + _dot(tri_b, la_lo)
        b_end = b[end:end + 1, :]
        qk = qk_ref[0, rs, :].astype(f32)
        q, k = qk[:, :GLA_QK_W], qk[:, GLA_QK_W:]
        q_dec = (q * jnp.exp(b)).astype(bf16)
        k_inv = (k * jnp.exp(-b)).astype(bf16)
        k_end = (k * jnp.exp(b_end - b)).astype(bf16)
        dcol = [jnp.exp(jnp.broadcast_to(b_end[:, lanes(p)], (8, LANES)).T[:, 0:1]) for p in range(npair)]
        return q_dec, k_inv, k_end, dcol

    def local(ci, q_dec, k_inv, k_end, dcol):
        rs = slice(ci * C, (ci + 1) * C)
        out = []
        for p in range(npair):
            kp = k_inv[:, lanes(p)]
            kst = jnp.concatenate([jnp.where(half == 0, kp, zero), jnp.where(half == 1, kp, zero)], axis=0)
            a = lax.dot_general(q_dec[:, lanes(p)], kst, _NT, preferred_element_type=f32)
            a = jnp.where(tri2, a, 0.0).astype(bf16)
            vs = slice(2 * p * GLA_DV, (2 * p + 2) * GLA_DV)
            v_even, v_odd = v_ref[0, rs, vs][:, :GLA_DV], v_ref[0, rs, vs][:, GLA_DV:]
            vbd = jnp.concatenate([jnp.concatenate([v_even, zv], axis=1),
                                   jnp.concatenate([zv, v_odd], axis=1)], axis=0)
            kv = lax.dot_general(k_end[:, lanes(p)], v_ref[0, rs, vs], _TN, preferred_element_type=f32)
            out.append((_dot(a, vbd), jnp.where(diag, kv, 0.0)))
        return out

    def carry(pairs, q_dec, dcol, st):
        out = []
        for p, (o2, kv) in enumerate(pairs):
            out.append(o2 + _dot(q_dec[:, lanes(p)], st[p].astype(bf16)))
            st[p] = st[p] * dcol[p] + kv
        return out

    def emit(ci, o2s):
        rs = slice(ci * C, (ci + 1) * C)
        for h in range(GLA_HEADS):
            vs = slice(h * GLA_DV, (h + 1) * GLA_DV)
            o = o2s[h // 2][:, (h % 2) * GLA_DV:(h % 2 + 1) * GLA_DV]
            if final:
                o = o + of_ref[0, rs, vs]
                o = o * lax.rsqrt(jnp.mean(o * o, axis=-1, keepdims=True) + EPS)
                o_ref[0, rs, vs] = (o * gate_ref[0, rs, vs].astype(f32)).astype(bf16)
            else:
                o_ref[0, rs, vs] = o

    st = [st_ref[p] for p in range(npair)]
    dc, lc, oc = {}, {}, {}
    for t in range(len(order) + GLA_LAG_L + GLA_LAG_C + GLA_LAG_E):
        if t < len(order):
            dc[t] = decays(order[t])
        n = t - GLA_LAG_L
        if 0 <= n < len(order):
            lc[n] = local(order[n], *dc[n])
        n = t - GLA_LAG_L - GLA_LAG_C
        if 0 <= n < len(order):
            q_dec, _, _, dcol = dc.pop(n)
            oc[n] = carry(lc.pop(n), q_dec, dcol, st)
        n = t - GLA_LAG_L - GLA_LAG_C - GLA_LAG_E
        if 0 <= n < len(order):
            emit(order[n], oc.pop(n))
    for p in range(npair):
        st_ref[p] = st[p]


def _gla(qk, v, la, gate):
    B, L, _ = qk.shape
    nf, nb = L // TL_FWD, L // TL_BWD
    scratch = [pltpu.VMEM((GLA_HEADS // 2, 2 * GLA_DK, 2 * GLA_DV), f32)]
    fwd = lambda w, cb=0: pl.BlockSpec((1, TL_FWD, w), lambda b, j: (b, j, cb))
    bwd = lambda w, cb=0: pl.BlockSpec((1, TL_BWD, w), lambda b, j: (b, nb - 1 - j, cb))
    o_f = pl.pallas_call(
        functools.partial(_gla_kernel, reverse=False, final=False),
        grid=(B, nf),
        in_specs=[fwd(2 * GLA_QK_W), fwd(GLA_V_W), fwd(GLA_QK_W, 0)],
        out_specs=fwd(GLA_V_W), out_shape=jax.ShapeDtypeStruct((B, L, GLA_V_W), f32),
        scratch_shapes=scratch, compiler_params=_params("parallel", "arbitrary"),
        name="gla_fwd",
    )(qk, v, la)
    return pl.pallas_call(
        functools.partial(_gla_kernel, reverse=True, final=True),
        grid=(B, nb),
        in_specs=[bwd(2 * GLA_QK_W), bwd(GLA_V_W), bwd(GLA_QK_W, 1), bwd(GLA_V_W), bwd(GLA_V_W)],
        out_specs=bwd(GLA_V_W), out_shape=jax.ShapeDtypeStruct((B, L, GLA_V_W), bf16),
        scratch_shapes=scratch, compiler_params=_params("parallel", "arbitrary"),
        name="gla_bwd",
    )(qk, v, la, o_f, gate)


def _att_kernel(q_ref, kv_ref, kvp_ref, kvn_ref, o_ref, lse_ref, vt_ref, qt_ref, ot_ref, bias_ref, *, n_seq, dil):
    i = pl.program_id(0)
    R = ATT_RADIUS
    npair = ATT_HEADS // 2

    @pl.when(i == 0)
    def _():
        key = lax.broadcasted_iota(jnp.int32, (KW, 2 * TQ), 0)
        qry = lax.broadcasted_iota(jnp.int32, (KW, 2 * TQ), 1) % TQ
        odd = lax.broadcasted_iota(jnp.int32, (KW, 2 * TQ), 1) >= TQ
        rel = jnp.abs(key - R - qry)
        relf = rel.astype(f32)
        for var in range(4):
            ok = rel <= R
            if var & 1:
                ok = ok & (key >= R)
            if var & 2:
                ok = ok & (key < R + TQ)
            for p in range(npair):
                s_even, s_odd = [2.0 ** (-8.0 * (h + 1) / ATT_HEADS) * dil * LOG2E for h in (2 * p, 2 * p + 1)]
                bias_ref[var * npair + p] = jnp.where(ok, -jnp.where(odd, s_odd, s_even) * relf, NEG)

    vt_ref[...] = jnp.concatenate([kvp_ref[:, ATT_W:], kv_ref[:, ATT_W:], kvn_ref[:, ATT_W:]], axis=0).T
    qt_ref[...] = q_ref[...].T

    def kwin(a, ps):
        lo, hi = a * TQ - R, a * TQ + TQ + R
        parts = [kvp_ref[:, ps]] if lo < 0 else []
        parts.append(kv_ref[max(lo, 0):min(hi, TB), ps])
        if hi > TB:
            parts.append(kvn_ref[:, ps])
        return jnp.concatenate(parts, axis=0) if len(parts) > 1 else parts[0]

    rhalf = lax.broadcasted_iota(jnp.int32, (LANES, TQ), 0) // ATT_DH
    sub = lax.broadcasted_iota(jnp.int32, (8, TQ), 0)
    zero = jnp.zeros((), bf16)
    def tree(op, x):
        parts = [x[r * 8:(r + 1) * 8] for r in range(KW // 8)]
        while len(parts) > 1:
            parts = [op(parts[k], parts[k + 1]) for k in range(0, len(parts), 2)]
        return parts[0]

    nsub = TB // TQ
    var = []
    for a in range(nsub):
        start = i * TB + a * TQ
        var.append((jnp.where((start % n_seq) == 0, 1, 0) + jnp.where(((start + TQ) % n_seq) == 0, 2, 0)) * npair)

    def scores(a, p):
        ps = slice(p * LANES, (p + 1) * LANES)
        qt = qt_ref[ps, a * TQ:(a + 1) * TQ]
        qst = jnp.concatenate([jnp.where(rhalf == 0, qt, zero), jnp.where(rhalf == 1, qt, zero)], axis=1)
        st = _dot(kwin(a, ps), qst) + bias_ref[var[a] + p]
        m = jnp.max(tree(jnp.maximum, st), axis=0, keepdims=True)
        return st, m

    def probs(st, m):
        pe = jnp.exp2(st - m)
        l = jnp.sum(tree(jnp.add, pe), axis=0, keepdims=True)
        return pe.astype(bf16), l

    def values(a, p, pt, l, m):
        ps = slice(p * LANES, (p + 1) * LANES)
        ot = _dot(vt_ref[ps, a * TQ:a * TQ + KW], pt) * (1.0 / l)
        ot_ref[ps, a * TQ:(a + 1) * TQ] = jnp.where(rhalf == 0, ot[:, :TQ], ot[:, TQ:])
        return m + jnp.log2(l)

    items = [(a, p) for a in range(nsub) for p in range(npair)]
    sc, pr, lse = {}, {}, {}
    for t in range(len(items) + ATT_LAG_P + ATT_LAG_V):
        if t < len(items):
            sc[t] = scores(*items[t])
        if 0 <= t - ATT_LAG_P < len(items):
            pr[t - ATT_LAG_P] = probs(*sc[t - ATT_LAG_P])
        n = t - ATT_LAG_P - ATT_LAG_V
        if 0 <= n < len(items):
            lse[n] = values(*items[n], *pr.pop(n), sc.pop(n)[1])
    for a in range(nsub):
        lse8 = jnp.zeros((8, TQ), f32)
        for p in range(npair):
            row = lse[a * npair + p]
            lse8 = jnp.where(sub == 2 * p, row[:, :TQ], jnp.where(sub == 2 * p + 1, row[:, TQ:], lse8))
        lse_ref[a * TQ:(a + 1) * TQ, :] = jnp.concatenate([lse8, jnp.zeros((LANES - 8, TQ), f32)], axis=0).T
    o_ref[...] = ot_ref[...].T.astype(bf16)


def _att(q, kv, n_seq, dil):
    rows = q.shape[0]
    nhalo = rows // ATT_RADIUS
    per = TB // ATT_RADIUS
    return pl.pallas_call(
        functools.partial(_att_kernel, n_seq=n_seq, dil=dil),
        grid=(rows // TB,),
        in_specs=[pl.BlockSpec((TB, ATT_W), lambda i: (i, 0)),
                  pl.BlockSpec((TB, 2 * ATT_W), lambda i: (i, 0)),
                  pl.BlockSpec((ATT_RADIUS, 2 * ATT_W), lambda i: (jnp.maximum(i * per - 1, 0), 0)),
                  pl.BlockSpec((ATT_RADIUS, 2 * ATT_W), lambda i: (jnp.minimum((i + 1) * per, nhalo - 1), 0))],
        out_specs=[pl.BlockSpec((TB, ATT_W), lambda i: (i, 0)), pl.BlockSpec((TB, LANES), lambda i: (i, 0))],
        out_shape=[jax.ShapeDtypeStruct((rows, ATT_W), bf16), jax.ShapeDtypeStruct((rows, LANES), f32)],
        scratch_shapes=[pltpu.VMEM((ATT_W, TB + 2 * ATT_RADIUS), bf16),
                        pltpu.VMEM((ATT_W, TB), bf16),
                        pltpu.VMEM((ATT_W, TB), f32),
                        pltpu.VMEM((4 * (ATT_HEADS // 2), KW, 2 * TQ), f32)],
        compiler_params=_params("arbitrary"),
        name=f"att_d{dil}",
    )(q, kv, kv, kv)


def _mix_ffn_kernel(h_ref, og_ref, o1_ref, l1_ref, o4_ref, l4_ref, o16_ref, l16_ref, wo_ref,
                    g_ref, wg_ref, wu_ref, wd_ref, gf_ref, out_ref, os_ref, ls_ref, cat_ref):
    ngrp = ATT_W // LANES
    nout = D_MODEL // MXU_N
    cols = lambda c: slice(c * MXU_N, (c + 1) * MXU_N)
    n4 = TM // 4

    og = og_ref[0]
    part = [h_ref[0, :, cols(c)] + _dot(og, wo_ref[:GLA_V_W, cols(c)]) for c in range(nout)]

    for c in range(16):
        c4, m = c % 4, c // 4
        ls_ref[2, pl.ds(c4 * n4 + m, n4 // 4, stride=4), :] = l16_ref[0, c]
        oc = o16_ref[0, c].astype(f32)
        for gi in range(ngrp):
            os_ref[2, gi, pl.ds(c4 * n4 + m, n4 // 4, stride=4), :] = oc[:, gi * LANES:(gi + 1) * LANES]
    for c in range(4):
        ls_ref[0, pl.ds(c, n4, stride=4), :] = l4_ref[0, c]
        ls_ref[1, pl.ds(c, n4, stride=4), :] = ls_ref[2, c * n4:(c + 1) * n4, :]
        oc = o4_ref[0, c].astype(f32)
        for gi in range(ngrp):
            os_ref[0, gi, pl.ds(c, n4, stride=4), :] = oc[:, gi * LANES:(gi + 1) * LANES]
            os_ref[1, gi, pl.ds(c, n4, stride=4), :] = os_ref[2, gi, c * n4:(c + 1) * n4, :]

    l1, l4, l16 = l1_ref[0], ls_ref[0], ls_ref[1]
    m = jnp.maximum(jnp.maximum(l1, l4), l16)
    w1, w4, w16 = jnp.exp2(l1 - m), jnp.exp2(l4 - m), jnp.exp2(l16 - m)
    inv = 1.0 / (w1 + w4 + w16)
    a1, a4, a16 = w1 * inv, w4 * inv, w16 * inv
    head = lax.broadcasted_iota(jnp.int32, (TM, LANES), 1) // ATT_DH
    for gi in range(ngrp):
        idx = head + 2 * gi
        gs = slice(gi * LANES, (gi + 1) * LANES)
        o = (jnp.take_along_axis(a1, idx, axis=1) * o1_ref[0, :, gs].astype(f32)
             + jnp.take_along_axis(a4, idx, axis=1) * os_ref[0, gi]
             + jnp.take_along_axis(a16, idx, axis=1) * os_ref[1, gi])
        cat_ref[:, gs] = o.astype(bf16)
    oa = cat_ref[...]
    h2 = jnp.concatenate([part[c] + _dot(oa, wo_ref[GLA_V_W:, cols(c)]) for c in range(nout)], axis=-1)
    out_ref[0] = _rms(_swiglu_residual(h2, g_ref, wg_ref, wu_ref, wd_ref), gf_ref[...])


def _mix_ffn(h, og, o1, l1, o4, l4, o16, l16, w_out, g, wg, wu, wd, g_final):
    B, L, D = h.shape
    tok = lambda w: pl.BlockSpec((1, TM, w), lambda b, t: (b, t, 0))
    cls = lambda d, w: pl.BlockSpec((1, d, TM // d, w), lambda b, t: (b, 0, t, 0))
    return pl.pallas_call(
        _mix_ffn_kernel,
        grid=(B, L // TM),
        in_specs=[tok(D), tok(GLA_V_W), tok(ATT_W), tok(LANES), cls(4, ATT_W), cls(4, LANES),
                  cls(16, ATT_W), cls(16, LANES), _const_spec((D, D))] + _ffn_specs(D) + [_const_spec((1, D))],
        out_specs=tok(D), out_shape=jax.ShapeDtypeStruct((B, L, D), f32),
        scratch_shapes=[pltpu.VMEM((3, ATT_W // LANES, TM, LANES), f32), pltpu.VMEM((3, TM, LANES), f32),
                        pltpu.VMEM((TM, ATT_W), bf16)],
        compiler_params=_params("parallel", "parallel"),
        name="mix_ffn",
    )(h, og, o1, l1, o4, l4, o16, l16, w_out, g, wg, wu, wd, g_final)


def _prep_layer(g_ffn1, w1_gate, w1_up, w1_down, g_mix, w_in, w_a2_fwd, b_a_fwd, w_a2_bwd, b_a_bwd,
                g_gla_out, w_out, g_ffn2, w2_gate, w2_up, w2_down):
    c = [0]
    for w in (GLA_QK_W, GLA_QK_W, GLA_V_W, GLA_V_W, GLA_RANK, GLA_RANK, ATT_W, ATT_W, ATT_W):
        c.append(c[-1] + w)
    gq, gk, gv, gr, lrf, lrb, aq, ak, av = [w_in[:, c[i]:c[i + 1]] for i in range(9)]
    pad = jnp.zeros((D_MODEL, LR_PAD - 2 * GLA_RANK), f32)
    w_gla = jnp.concatenate([gq * GLA_DK ** -0.5, gk, gv, gr, lrf, lrb, pad], axis=1).astype(bf16)
    w_att = jnp.concatenate([aq * (ATT_DH ** -0.5 * LOG2E), ak, av], axis=1).astype(bf16)
    wa = jnp.zeros((LR_PAD, 2 * GLA_QK_W), f32)
    wa = wa.at[:GLA_RANK, :GLA_QK_W].set(w_a2_fwd).at[GLA_RANK:2 * GLA_RANK, GLA_QK_W:].set(w_a2_bwd)
    wa_hi = wa.astype(bf16)
    wa_lo = (wa - wa_hi.astype(f32)).astype(bf16)
    ba = jnp.concatenate([b_a_fwd, b_a_bwd])[None, :]
    return dict(
        ffn1=(g_ffn1[None, :], w1_gate.astype(bf16), w1_up.astype(bf16), w1_down.astype(bf16)),
        proj=(g_mix[None, :], w_gla, w_att, wa_hi, wa_lo, ba, g_gla_out[None, :]),
        w_out=w_out.astype(bf16),
        ffn2=(g_ffn2[None, :], w2_gate.astype(bf16), w2_up.astype(bf16), w2_down.astype(bf16)))


def _layer(x, p, g_final):
    B, L, _ = x.shape
    h = _ffn(x, *p["ffn1"])
    qk, v, r, la, q1, kv1, q4, kv4, q16, kv16 = _proj(h, *p["proj"])
    og = _gla(qk, v, la, r)
    outs = []
    for d, q, kv in ((1, q1, kv1), (4, q4, kv4), (16, q16, kv16)):
        o, lse = _att(q.reshape(B * L, ATT_W), kv.reshape(B * L, 2 * ATT_W), L // d, d)
        shape = (B, L) if d == 1 else (B, d, L // d)
        outs += [o.reshape(*shape, ATT_W), lse.reshape(*shape, LANES)]
    return _mix_ffn(h, og, *outs, p["w_out"], *p["ffn2"], g_final)


def kernel(x_prompt, x_sample, g_ffn1, w1_gate, w1_up, w1_down, g_mix, w_in, w_a2_fwd, b_a_fwd, w_a2_bwd, b_a_bwd, g_gla_out, w_out, g_ffn2, w2_gate, w2_up, w2_down, g_final):
    depth = g_ffn1.shape[0]
    assert depth == 1, "final norm is fused into the last layer's second FFN"
    p = _prep_layer(g_ffn1[0], w1_gate[0], w1_up[0], w1_down[0], g_mix[0], w_in[0], w_a2_fwd[0], b_a_fwd[0],
                    w_a2_bwd[0], b_a_bwd[0], g_gla_out[0], w_out[0], g_ffn2[0], w2_gate[0], w2_up[0], w2_down[0])
    gf = g_final[None, :]
    return (_layer(x_prompt, p, gf), _layer(x_sample, p, gf))
```

```python
import functools

import jax
import jax.numpy as jnp
from jax import lax
from jax.experimental import pallas as pl
from jax.experimental.pallas import tpu as pltpu

f32 = jnp.float32
bf16 = jnp.bfloat16

D_MODEL = 1024
D_FF = 2816
GLA_HEADS = 4
GLA_DK = 64
GLA_DV = 128
GLA_RANK = 16
GLA_TAU = 16.0
GLA_CHUNK = 64
GLA_QK_W = GLA_HEADS * GLA_DK
GLA_V_W = GLA_HEADS * GLA_DV
ATT_HEADS = 8
ATT_DH = 64
ATT_W = ATT_HEADS * ATT_DH
ATT_RADIUS = 64
DILATIONS = (1, 4, 16)
EPS = 1e-6
NEG = -1e30

LOG2E = 1.4426950408889634

LANES = 128
MXU_N = 256
TM = 512
TM_FFN = 1024
FF_CHUNK = MXU_N
TL_FWD = 1024
TL_BWD = 512
TB = 2048
TQ = 128
KW = TQ + 2 * ATT_RADIUS
GLA_LAG_L = 2
GLA_LAG_C = 2
GLA_LAG_E = 4
ATT_LAG_P = 2
ATT_LAG_V = 2
VMEM_LIMIT = 56 * 1024 * 1024

LR_PAD = LANES
Z_GLA_W = 2 * GLA_QK_W + 2 * GLA_V_W + LR_PAD
Z_ATT_W = 3 * ATT_W

_NT = (((1,), (1,)), ((), ()))
_TN = (((0,), (0,)), ((), ()))


def _dot(a, b):
    return jnp.dot(a, b, preferred_element_type=f32)


def _rms(x, g):
    return x * lax.rsqrt(jnp.mean(x * x, axis=-1, keepdims=True) + EPS) * g


def _split(x):
    hi = x.astype(bf16)
    lo = (x - hi.astype(f32)).astype(bf16)
    return hi, lo


def _const_spec(shape):
    zeros = (0,) * len(shape)
    return pl.BlockSpec(shape, lambda *_: zeros, pipeline_mode=pl.Buffered(1))


def _params(*sem):
    return pltpu.CompilerParams(dimension_semantics=sem, vmem_limit_bytes=VMEM_LIMIT)


def _swiglu_residual(x, g_ref, wg_ref, wu_ref, wd_ref):
    xn = _rms(x, g_ref[...]).astype(bf16)
    acc = jnp.zeros(x.shape, f32)
    for c in range(D_FF // FF_CHUNK):
        sl = slice(c * FF_CHUNK, (c + 1) * FF_CHUNK)
        g = _dot(xn, wg_ref[:, sl])
        u = _dot(xn, wu_ref[:, sl])
        a = (g * (1.0 / (1.0 + jnp.exp(-g))) * u).astype(bf16)
        acc = acc + _dot(a, wd_ref[sl, :])
    return x + 0.5 * acc


def _ffn_kernel(x_ref, g_ref, wg_ref, wu_ref, wd_ref, o_ref):
    o_ref[0] = _swiglu_residual(x_ref[0], g_ref, wg_ref, wu_ref, wd_ref)


def _ffn_specs(D):
    return [_const_spec((1, D)), _const_spec((D, D_FF)), _const_spec((D, D_FF)), _const_spec((D_FF, D))]


def _ffn(x, g, wg, wu, wd):
    B, L, D = x.shape
    tok = pl.BlockSpec((1, TM_FFN, D), lambda b, t: (b, t, 0))
    return pl.pallas_call(
        _ffn_kernel,
        grid=(B, L // TM_FFN), in_specs=[tok] + _ffn_specs(D), out_specs=tok,
        out_shape=jax.ShapeDtypeStruct((B, L, D), f32),
        compiler_params=_params("parallel", "parallel"),
        name="ffn",
    )(x, g, wg, wu, wd)


def _proj_kernel(h_ref, g_ref, wg_ref, wa_ref, w2_ref, ba_ref, go_ref,
                 qk_ref, v_ref, r_ref, la_ref, q1_ref, kv1_ref, q4_ref, kv4_ref, q16_ref, kv16_ref,
                 zs_ref, z4_ref):
    un = _rms(h_ref[0], g_ref[...]).astype(bf16)
    ngrp = Z_ATT_W // LANES
    per = MXU_N // LANES
    n4 = TM // 4

    def att_cols(c):
        z = _dot(un, wa_ref[:, c * MXU_N:(c + 1) * MXU_N])
        if c * MXU_N < ATT_W:
            q1_ref[0, :, c * MXU_N:(c + 1) * MXU_N] = z.astype(bf16)
        else:
            kv1_ref[0, :, c * MXU_N - ATT_W:(c + 1) * MXU_N - ATT_W] = z.astype(bf16)
        for k in range(per):
            zs_ref[c * per + k] = z[:, k * LANES:(k + 1) * LANES]

    def put(q_ref, kv_ref, c, gi, rows):
        col = gi * LANES
        if col < ATT_W:
            q_ref[0, c, :, col:col + LANES] = rows.astype(bf16)
        else:
            kv_ref[0, c, :, col - ATT_W:col - ATT_W + LANES] = rows.astype(bf16)

    def permute4(classes):
        for c in classes:
            for gi in range(ngrp):
                rows = zs_ref[gi, pl.ds(c, n4, stride=4), :]
                z4_ref[gi, c * n4:(c + 1) * n4, :] = rows
                put(q4_ref, kv4_ref, c, gi, rows)

    def permute16(classes):
        for c4 in classes:
            for m in range(4):
                for gi in range(ngrp):
                    put(q16_ref, kv16_ref, c4 + 4 * m, gi, z4_ref[gi, pl.ds(c4 * n4 + m, n4 // 4, stride=4), :])

    def gla_cols(c):
        ref, off = ((qk_ref, 0), (qk_ref, MXU_N), (v_ref, 0), (v_ref, MXU_N), (r_ref, 0), (r_ref, MXU_N))[c]
        z = _dot(un, wg_ref[:, c * MXU_N:(c + 1) * MXU_N])
        if ref is r_ref:
            z = z * (1.0 / (1.0 + jnp.exp(-z))) * go_ref[:, off:off + MXU_N]
        ref[0, :, off:off + MXU_N] = z.astype(bf16)

    lr = _dot(un, wg_ref[:, Z_GLA_W - LR_PAD:])
    hi = lr.astype(bf16).astype(f32)
    packed = hi + pltpu.roll(lr - hi, 2 * GLA_RANK, axis=1) + pltpu.roll(hi, 4 * GLA_RANK, axis=1)
    att_cols(0)
    pre = _dot(packed.astype(bf16), w2_ref[...]) + ba_ref[...]
    att_cols(1)
    att_cols(2)
    la_ref[0] = (jnp.minimum(pre, 0.0) - jnp.log(1.0 + jnp.exp(-jnp.abs(pre)))) * (1.0 / GLA_TAU)
    for c in range(3, Z_ATT_W // MXU_N):
        att_cols(c)
    gla_cols(0)
    permute4(range(0, 2))
    gla_cols(1)
    permute4(range(2, 4))
    for c in range(2, 6):
        gla_cols(c)
        permute16(range(c - 2, c - 1))


def _proj(h, g, w_gla, w_att, w_gate, ba, g_out):
    B, L, D = h.shape
    tok = lambda w: pl.BlockSpec((1, TM, w), lambda b, t: (b, t, 0))
    cls = lambda d, w: pl.BlockSpec((1, d, TM // d, w), lambda b, t: (b, 0, t, 0))
    sds = jax.ShapeDtypeStruct
    out_shape = [sds((B, L, 2 * GLA_QK_W), bf16), sds((B, L, GLA_V_W), bf16), sds((B, L, GLA_V_W), bf16),
                 sds((B, L, 2 * GLA_QK_W), f32),
                 sds((B, L, ATT_W), bf16), sds((B, L, 2 * ATT_W), bf16)]
    out_specs = [tok(2 * GLA_QK_W), tok(GLA_V_W), tok(GLA_V_W), tok(2 * GLA_QK_W), tok(ATT_W), tok(2 * ATT_W)]
    for d in DILATIONS[1:]:
        out_shape += [sds((B, d, L // d, ATT_W), bf16), sds((B, d, L // d, 2 * ATT_W), bf16)]
        out_specs += [cls(d, ATT_W), cls(d, 2 * ATT_W)]
    return pl.pallas_call(
        _proj_kernel,
        grid=(B, L // TM),
        in_specs=[tok(D), _const_spec((1, D)), _const_spec((D, Z_GLA_W)), _const_spec((D, Z_ATT_W)),
                  _const_spec((LR_PAD, 2 * GLA_QK_W)), _const_spec((1, 2 * GLA_QK_W)), _const_spec((1, GLA_V_W))],
        out_specs=out_specs, out_shape=out_shape,
        scratch_shapes=[pltpu.VMEM((Z_ATT_W // LANES, TM, LANES), f32)] * 2,
        compiler_params=_params("parallel", "parallel"),
        name="proj",
    )(h, g, w_gla, w_att, w_gate, ba, g_out)


def _gla_kernel(qk_ref, v_ref, la_ref, *rest, reverse, final):
    if final:
        of_ref, gate_ref, o_ref, st_ref = rest
    else:
        o_ref, st_ref = rest
    C = GLA_CHUNK
    npair = GLA_HEADS // 2
    zero = jnp.zeros((), bf16)

    @pl.when(pl.program_id(1) == 0)
    def _():
        st_ref[...] = jnp.zeros(st_ref.shape, f32)

    row = lax.broadcasted_iota(jnp.int32, (C, C), 0)
    col = lax.broadcasted_iota(jnp.int32, (C, C), 1)
    tri_b = ((col >= row) if reverse else (col <= row)).astype(bf16)
    row2 = lax.broadcasted_iota(jnp.int32, (C, 2 * C), 0)
    col2 = lax.broadcasted_iota(jnp.int32, (C, 2 * C), 1) % C
    tri2 = (col2 >= row2) if reverse else (col2 <= row2)
    half = lax.broadcasted_iota(jnp.int32, (C, LANES), 1) // GLA_DK
    diag = (lax.broadcasted_iota(jnp.int32, (LANES, 2 * GLA_DV), 0) // GLA_DK
            == lax.broadcasted_iota(jnp.int32, (LANES, 2 * GLA_DV), 1) // GLA_DV)
    zv = jnp.zeros((C, GLA_DV), bf16)
    end = 0 if reverse else C - 1
    nchunk = qk_ref.shape[1] // C
    order = list(range(nchunk - 1, -1, -1) if reverse else range(nchunk))
    lanes = lambda p: slice(p * LANES, (p + 1) * LANES)

    def decays(ci):
        rs = slice(ci * C, (ci + 1) * C)
        la_hi, la_lo = _split(la_ref[0, rs, :])
        b = _dot(tri_b, la_hi) + _dot(tri_b, la_lo)
        b_end = b[end:end + 1, :]
        qk = qk_ref[0, rs, :].astype(f32)
        q, k = qk[:, :GLA_QK_W], qk[:, GLA_QK_W:]
        q_dec = (q * jnp.exp(b)).astype(bf16)
        k_inv = (k * jnp.exp(-b)).astype(bf16)
        k_end = (k * jnp.exp(b_end - b)).astype(bf16)
        dcol = [jnp.exp(jnp.broadcast_to(b_end[:, lanes(p)], (8, LANES)).T[:, 0:1]) for p in range(npair)]
        return q_dec, k_inv, k_end, dcol

    def local(ci, q_dec, k_inv, k_end, dcol):
        rs = slice(ci * C, (ci + 1) * C)
        out = []
        for p in range(npair):
            kp = k_inv[:, lanes(p)]
            kst = jnp.concatenate([jnp.where(half == 0, kp, zero), jnp.where(half == 1, kp, zero)], axis=0)
            a = lax.dot_general(q_dec[:, lanes(p)], kst, _NT, preferred_element_type=f32)
            a = jnp.where(tri2, a, 0.0).astype(bf16)
            vs = slice(2 * p * GLA_DV, (2 * p + 2) * GLA_DV)
            v_even, v_odd = v_ref[0, rs, vs][:, :GLA_DV], v_ref[0, rs, vs][:, GLA_DV:]
            vbd = jnp.concatenate([jnp.concatenate([v_even, zv], axis=1),
                                   jnp.concatenate([zv, v_odd], axis=1)], axis=0)
            kv = lax.dot_general(k_end[:, lanes(p)], v_ref[0, rs, vs], _TN, preferred_element_type=f32)
            out.append((_dot(a, vbd), jnp.where(diag, kv, 0.0)))
        return out

    def carry(pairs, q_dec, dcol, st):
        out = []
        for p, (o2, kv) in enumerate(pairs):
            out.append(o2 + _dot(q_dec[:, lanes(p)], st[p].astype(bf16)))
            st[p] = st[p] * dcol[p] + kv
        return out

    def emit(ci, o2s):
        rs = slice(ci * C, (ci + 1) * C)
        for h in range(GLA_HEADS):
            vs = slice(h * GLA_DV, (h + 1) * GLA_DV)
            o = o2s[h // 2][:, (h % 2) * GLA_DV:(h % 2 + 1) * GLA_DV]
            if final:
                o = o + of_ref[0, rs, vs]
                o = o * lax.rsqrt(jnp.mean(o * o, axis=-1, keepdims=True) + EPS)
                o_ref[0, rs, vs] = (o * gate_ref[0, rs, vs].astype(f32)).astype(bf16)
            else:
                o_ref[0, rs, vs] = o

    st = [st_ref[p] for p in range(npair)]
    dc, lc, oc = {}, {}, {}
    for t in range(len(order) + GLA_LAG_L + GLA_LAG_C + GLA_LAG_E):
        if t < len(order):
            dc[t] = decays(order[t])
        n = t - GLA_LAG_L
        if 0 <= n < len(order):
            lc[n] = local(order[n], *dc[n])
        n = t - GLA_LAG_L - GLA_LAG_C
        if 0 <= n < len(order):
            q_dec, _, _, dcol = dc.pop(n)
            oc[n] = carry(lc.pop(n), q_dec, dcol, st)
        n = t - GLA_LAG_L - GLA_LAG_C - GLA_LAG_E
        if 0 <= n < len(order):
            emit(order[n], oc.pop(n))
    for p in range(npair):
        st_ref[p] = st[p]


def _gla(qk, v, la, gate):
    B, L, _ = qk.shape
    nf, nb = L // TL_FWD, L // TL_BWD
    scratch = [pltpu.VMEM((GLA_HEADS // 2, 2 * GLA_DK, 2 * GLA_DV), f32)]
    fwd = lambda w, cb=0: pl.BlockSpec((1, TL_FWD, w), lambda b, j: (b, j, cb))
    bwd = lambda w, cb=0: pl.BlockSpec((1, TL_BWD, w), lambda b, j: (b, nb - 1 - j, cb))
    o_f = pl.pallas_call(
        functools.partial(_gla_kernel, reverse=False, final=False),
        grid=(B, nf),
        in_specs=[fwd(2 * GLA_QK_W), fwd(GLA_V_W), fwd(GLA_QK_W, 0)],
        out_specs=fwd(GLA_V_W), out_shape=jax.ShapeDtypeStruct((B, L, GLA_V_W), f32),
        scratch_shapes=scratch, compiler_params=_params("parallel", "arbitrary"),
        name="gla_fwd",
    )(qk, v, la)
    return pl.pallas_call(
        functools.partial(_gla_kernel, reverse=True, final=True),
        grid=(B, nb),
        in_specs=[bwd(2 * GLA_QK_W), bwd(GLA_V_W), bwd(GLA_QK_W, 1), bwd(GLA_V_W), bwd(GLA_V_W)],
        out_specs=bwd(GLA_V_W), out_shape=jax.ShapeDtypeStruct((B, L, GLA_V_W), bf16),
        scratch_shapes=scratch, compiler_params=_params("parallel", "arbitrary"),
        name="gla_bwd",
    )(qk, v, la, o_f, gate)


def _att_kernel(q_ref, kv_ref, kvp_ref, kvn_ref, o_ref, lse_ref, vt_ref, qt_ref, ot_ref, bias_ref, *, n_seq, dil):
    i = pl.program_id(0)
    R = ATT_RADIUS
    npair = ATT_HEADS // 2

    @pl.when(i == 0)
    def _():
        key = lax.broadcasted_iota(jnp.int32, (KW, 2 * TQ), 0)
        qry = lax.broadcasted_iota(jnp.int32, (KW, 2 * TQ), 1) % TQ
        odd = lax.broadcasted_iota(jnp.int32, (KW, 2 * TQ), 1) >= TQ
        rel = jnp.abs(key - R - qry)
        relf = rel.astype(f32)
        for var in range(4):
            ok = rel <= R
            if var & 1:
                ok = ok & (key >= R)
            if var & 2:
                ok = ok & (key < R + TQ)
            for p in range(npair):
                s_even, s_odd = [2.0 ** (-8.0 * (h + 1) / ATT_HEADS) * dil * LOG2E for h in (2 * p, 2 * p + 1)]
                bias_ref[var * npair + p] = jnp.where(ok, -jnp.where(odd, s_odd, s_even) * relf, NEG)

    vt_ref[...] = jnp.concatenate([kvp_ref[:, ATT_W:], kv_ref[:, ATT_W:], kvn_ref[:, ATT_W:]], axis=0).T
    qt_ref[...] = q_ref[...].T

    def kwin(a, ps):
        lo, hi = a * TQ - R, a * TQ + TQ + R
        parts = [kvp_ref[:, ps]] if lo < 0 else []
        parts.append(kv_ref[max(lo, 0):min(hi, TB), ps])
        if hi > TB:
            parts.append(kvn_ref[:, ps])
        return jnp.concatenate(parts, axis=0) if len(parts) > 1 else parts[0]

    rhalf = lax.broadcasted_iota(jnp.int32, (LANES, TQ), 0) // ATT_DH
    sub = lax.broadcasted_iota(jnp.int32, (8, TQ), 0)
    zero = jnp.zeros((), bf16)

    def tree(op, x):
        parts = [x[r * 8:(r + 1) * 8] for r in range(KW // 8)]
        while len(parts) > 1:
            parts = [op(parts[k], parts[k + 1]) for k in range(0, len(parts), 2)]
        return parts[0]

    nsub = TB // TQ
    var = []
    for a in range(nsub):
        start = i * TB + a * TQ
        var.append((jnp.where((start % n_seq) == 0, 1, 0) + jnp.where(((start + TQ) % n_seq) == 0, 2, 0)) * npair)

    def scores(a, p):
        ps = slice(p * LANES, (p + 1) * LANES)
        qt = qt_ref[ps, a * TQ:(a + 1) * TQ]
        qst = jnp.concatenate([jnp.where(rhalf == 0, qt, zero), jnp.where(rhalf == 1, qt, zero)], axis=1)
        st = _dot(kwin(a, ps), qst) + bias_ref[var[a] + p]
        m = jnp.max(tree(jnp.maximum, st), axis=0, keepdims=True)
        return st, m

    def probs(st, m):
        pe = jnp.exp2(st - m)
        l = jnp.sum(tree(jnp.add, pe), axis=0, keepdims=True)
        return pe.astype(bf16), l

    def values(a, p, pt, l, m):
        ps = slice(p * LANES, (p + 1) * LANES)
        ot = _dot(vt_ref[ps, a * TQ:a * TQ + KW], pt) * (1.0 / l)
        ot_ref[ps, a * TQ:(a + 1) * TQ] = jnp.where(rhalf == 0, ot[:, :TQ], ot[:, TQ:])
        return m + jnp.log2(l)

    items = [(a, p) for a in range(nsub) for p in range(npair)]
    sc, pr, lse = {}, {}, {}
    for t in range(len(items) + ATT_LAG_P + ATT_LAG_V):
        if t < len(items):
            sc[t] = scores(*items[t])
        if 0 <= t - ATT_LAG_P < len(items):
            pr[t - ATT_LAG_P] = probs(*sc[t - ATT_LAG_P])
        n = t - ATT_LAG_P - ATT_LAG_V
        if 0 <= n < len(items):
            lse[n] = values(*items[n], *pr.pop(n), sc.pop(n)[1])
    for a in range(nsub):
        lse8 = jnp.zeros((8, TQ), f32)
        for p in range(npair):
            row = lse[a * npair + p]
            lse8 = jnp.where(sub == 2 * p, row[:, :TQ], jnp.where(sub == 2 * p + 1, row[:, TQ:], lse8))
        lse_ref[a * TQ:(a + 1) * TQ, :] = jnp.concatenate([lse8, jnp.zeros((LANES - 8, TQ), f32)], axis=0).T
    o_ref[...] = ot_ref[...].T.astype(bf16)


def _att(q, kv, n_seq, dil):
    rows = q.shape[0]
    nhalo = rows // ATT_RADIUS
    per = TB // ATT_RADIUS
    return pl.pallas_call(
        functools.partial(_att_kernel, n_seq=n_seq, dil=dil),
        grid=(rows // TB,),
        in_specs=[pl.BlockSpec((TB, ATT_W), lambda i: (i, 0)),
                  pl.BlockSpec((TB, 2 * ATT_W), lambda i: (i, 0)),
                  pl.BlockSpec((ATT_RADIUS, 2 * ATT_W), lambda i: (jnp.maximum(i * per - 1, 0), 0)),
                  pl.BlockSpec((ATT_RADIUS, 2 * ATT_W), lambda i: (jnp.minimum((i + 1) * per, nhalo - 1), 0))],
        out_specs=[pl.BlockSpec((TB, ATT_W), lambda i: (i, 0)), pl.BlockSpec((TB, LANES), lambda i: (i, 0))],
        out_shape=[jax.ShapeDtypeStruct((rows, ATT_W), bf16), jax.ShapeDtypeStruct((rows, LANES), f32)],
        scratch_shapes=[pltpu.VMEM((ATT_W, TB + 2 * ATT_RADIUS), bf16),
                        pltpu.VMEM((ATT_W, TB), bf16),
                        pltpu.VMEM((ATT_W, TB), f32),
                        pltpu.VMEM((4 * (ATT_HEADS // 2), KW, 2 * TQ), f32)],
        compiler_params=_params("arbitrary"),
        name=f"att_d{dil}",
    )(q, kv, kv, kv)


def _mix_ffn_kernel(h_ref, og_ref, o1_ref, l1_ref, o4_ref, l4_ref, o16_ref, l16_ref, wo_ref,
                    g_ref, wg_ref, wu_ref, wd_ref, gf_ref, out_ref, os_ref, ls_ref, cat_ref):
    ngrp = ATT_W // LANES
    nout = D_MODEL // MXU_N
    cols = lambda c: slice(c * MXU_N, (c + 1) * MXU_N)
    n4 = TM // 4

    og = og_ref[0]
    part = [h_ref[0, :, cols(c)] + _dot(og, wo_ref[:GLA_V_W, cols(c)]) for c in range(nout)]

    for c in range(16):
        c4, m = c % 4, c // 4
        ls_ref[2, pl.ds(c4 * n4 + m, n4 // 4, stride=4), :] = l16_ref[0, c]
        oc = o16_ref[0, c].astype(f32)
        for gi in range(ngrp):
            os_ref[2, gi, pl.ds(c4 * n4 + m, n4 // 4, stride=4), :] = oc[:, gi * LANES:(gi + 1) * LANES]
    for c in range(4):
        ls_ref[0, pl.ds(c, n4, stride=4), :] = l4_ref[0, c]
        ls_ref[1, pl.ds(c, n4, stride=4), :] = ls_ref[2, c * n4:(c + 1) * n4, :]
        oc = o4_ref[0, c].astype(f32)
        for gi in range(ngrp):
            os_ref[0, gi, pl.ds(c, n4, stride=4), :] = oc[:, gi * LANES:(gi + 1) * LANES]
            os_ref[1, gi, pl.ds(c, n4, stride=4), :] = os_ref[2, gi, c * n4:(c + 1) * n4, :]

    l1, l4, l16 = l1_ref[0], ls_ref[0], ls_ref[1]
    m = jnp.maximum(jnp.maximum(l1, l4), l16)
    w1, w4, w16 = jnp.exp2(l1 - m), jnp.exp2(l4 - m), jnp.exp2(l16 - m)
    inv = 1.0 / (w1 + w4 + w16)
    a1, a4, a16 = w1 * inv, w4 * inv, w16 * inv
    head = lax.broadcasted_iota(jnp.int32, (TM, LANES), 1) // ATT_DH
    for gi in range(ngrp):
        idx = head + 2 * gi
        gs = slice(gi * LANES, (gi + 1) * LANES)
        o = (jnp.take_along_axis(a1, idx, axis=1) * o1_ref[0, :, gs].astype(f32)
             + jnp.take_along_axis(a4, idx, axis=1) * os_ref[0, gi]
             + jnp.take_along_axis(a16, idx, axis=1) * os_ref[1, gi])
        cat_ref[:, gs] = o.astype(bf16)
    oa = cat_ref[...]
    h2 = jnp.concatenate([part[c] + _dot(oa, wo_ref[GLA_V_W:, cols(c)]) for c in range(nout)], axis=-1)
    out_ref[0] = _rms(_swiglu_residual(h2, g_ref, wg_ref, wu_ref, wd_ref), gf_ref[...])


def _mix_ffn(h, og, o1, l1, o4, l4, o16, l16, w_out, g, wg, wu, wd, g_final):
    B, L, D = h.shape
    tok = lambda w: pl.BlockSpec((1, TM, w), lambda b, t: (b, t, 0))
    cls = lambda d, w: pl.BlockSpec((1, d, TM // d, w), lambda b, t: (b, 0, t, 0))
    return pl.pallas_call(
        _mix_ffn_kernel,
        grid=(B, L // TM),
        in_specs=[tok(D), tok(GLA_V_W), tok(ATT_W), tok(LANES), cls(4, ATT_W), cls(4, LANES),
                  cls(16, ATT_W), cls(16, LANES), _const_spec((D, D))] + _ffn_specs(D) + [_const_spec((1, D))],
        out_specs=tok(D), out_shape=jax.ShapeDtypeStruct((B, L, D), f32),
        scratch_shapes=[pltpu.VMEM((3, ATT_W // LANES, TM, LANES), f32), pltpu.VMEM((3, TM, LANES), f32),
                        pltpu.VMEM((TM, ATT_W), bf16)],
        compiler_params=_params("parallel", "parallel"),
        name="mix_ffn",
    )(h, og, o1, l1, o4, l4, o16, l16, w_out, g, wg, wu, wd, g_final)


def _prep_layer(g_ffn1, w1_gate, w1_up, w1_down, g_mix, w_in, w_a2_fwd, b_a_fwd, w_a2_bwd, b_a_bwd,
                g_gla_out, w_out, g_ffn2, w2_gate, w2_up, w2_down):
    c = [0]
    for w in (GLA_QK_W, GLA_QK_W, GLA_V_W, GLA_V_W, GLA_RANK, GLA_RANK, ATT_W, ATT_W, ATT_W):
        c.append(c[-1] + w)
    gq, gk, gv, gr, lrf, lrb, aq, ak, av = [w_in[:, c[i]:c[i + 1]] for i in range(9)]
    pad = jnp.zeros((D_MODEL, LR_PAD - 2 * GLA_RANK), f32)
    w_gla = jnp.concatenate([gq * GLA_DK ** -0.5, gk, gv, gr, lrf, lrb, pad], axis=1).astype(bf16)
    w_att = jnp.concatenate([aq * (ATT_DH ** -0.5 * LOG2E), ak, av], axis=1).astype(bf16)
    wa = jnp.zeros((2 * GLA_RANK, 2 * GLA_QK_W), f32)
    wa = wa.at[:GLA_RANK, :GLA_QK_W].set(w_a2_fwd).at[GLA_RANK:, GLA_QK_W:].set(w_a2_bwd)
    wa_hi = wa.astype(bf16)
    wa_lo = (wa - wa_hi.astype(f32)).astype(bf16)
    w_gate = jnp.concatenate([wa_hi, wa_hi, wa_lo, jnp.zeros_like(wa_hi)], axis=0)
    ba = jnp.concatenate([b_a_fwd, b_a_bwd])[None, :]
    return dict(
        ffn1=(g_ffn1[None, :], w1_gate.astype(bf16), w1_up.astype(bf16), w1_down.astype(bf16)),
        proj=(g_mix[None, :], w_gla, w_att, w_gate, ba, g_gla_out[None, :]),
        w_out=w_out.astype(bf16),
        ffn2=(g_ffn2[None, :], w2_gate.astype(bf16), w2_up.astype(bf16), w2_down.astype(bf16)))


def _layer(x, p, g_final):
    B, L, _ = x.shape
    h = _ffn(x, *p["ffn1"])
    qk, v, r, la, q1, kv1, q4, kv4, q16, kv16 = _proj(h, *p["proj"])
    og = _gla(qk, v, la, r)
    outs = []
    for d, q, kv in ((1, q1, kv1), (4, q4, kv4), (16, q16, kv16)):
        o, lse = _att(q.reshape(B * L, ATT_W), kv.reshape(B * L, 2 * ATT_W), L // d, d)
        shape = (B, L) if d == 1 else (B, d, L // d)
        outs += [o.reshape(*shape, ATT_W), lse.reshape(*shape, LANES)]
    return _mix_ffn(h, og, *outs, p["w_out"], *p["ffn2"], g_final)


def kernel(x_prompt, x_sample, g_ffn1, w1_gate, w1_up, w1_down, g_mix, w_in, w_a2_fwd, b_a_fwd, w_a2_bwd, b_a_bwd, g_gla_out, w_out, g_ffn2, w2_gate, w2_up, w2_down, g_final):
    depth = g_ffn1.shape[0]
    assert depth == 1, "final norm is fused into the last layer's second FFN"
    p = _prep_layer(g_ffn1[0], w1_gate[0], w1_up[0], w1_down[0], g_mix[0], w_in[0], w_a2_fwd[0], b_a_fwd[0],
                    w_a2_bwd[0], b_a_bwd[0], g_gla_out[0], w_out[0], g_ffn2[0], w2_gate[0], w2_up[0], w2_down[0])
    gf = g_final[None, :]
    return (_layer(x_prompt, p, gf), _layer(x_sample, p, gf))
```

```python
import functools

import jax
import jax.numpy as jnp
from jax import lax
from jax.experimental import pallas as pl
from jax.experimental.pallas import tpu as pltpu

f32 = jnp.float32
bf16 = jnp.bfloat16

D_MODEL = 1024
D_FF = 2816
GLA_HEADS = 4
GLA_DK = 64
GLA_DV = 128
GLA_RANK = 16
GLA_TAU = 16.0
GLA_CHUNK = 64
GLA_QK_W = GLA_HEADS * GLA_DK
GLA_V_W = GLA_HEADS * GLA_DV
ATT_HEADS = 8
ATT_DH = 64
ATT_W = ATT_HEADS * ATT_DH
ATT_RADIUS = 64
DILATIONS = (1, 4, 16)
EPS = 1e-6
NEG = -1e30

LOG2E = 1.4426950408889634

LANES = 128
MXU_N = 256
TM = 512
TM_FFN = 1024
FF_CHUNK = MXU_N
TL_FWD = 1024
TL_BWD = 512
TB = 2048
TQ = 128
KW = TQ + 2 * ATT_RADIUS
GLA_LAG_L = 2
GLA_LAG_C = 2
GLA_LAG_E = 4
ATT_LAG_P = 2
ATT_LAG_V = 2
VMEM_LIMIT = 56 * 1024 * 1024

LR_PAD = LANES
Z_GLA_W = 2 * GLA_QK_W + 2 * GLA_V_W + LR_PAD
Z_ATT_W = 3 * ATT_W

_NT = (((1,), (1,)), ((), ()))
_TN = (((0,), (0,)), ((), ()))


def _dot(a, b):
    return jnp.dot(a, b, preferred_element_type=f32)


def _rms(x, g):
    return x * lax.rsqrt(jnp.mean(x * x, axis=-1, keepdims=True) + EPS) * g


def _split(x):
    hi = x.astype(bf16)
    lo = (x - hi.astype(f32)).astype(bf16)
    return hi, lo


def _const_spec(shape):
    zeros = (0,) * len(shape)
    return pl.BlockSpec(shape, lambda *_: zeros, pipeline_mode=pl.Buffered(1))


def _params(*sem):
    return pltpu.CompilerParams(dimension_semantics=sem, vmem_limit_bytes=VMEM_LIMIT)


def _swiglu_residual(x, g_ref, wg_ref, wu_ref, wd_ref):
    xn = _rms(x, g_ref[...]).astype(bf16)
    acc = jnp.zeros(x.shape, f32)
    for c in range(D_FF // FF_CHUNK):
        sl = slice(c * FF_CHUNK, (c + 1) * FF_CHUNK)
        g = _dot(xn, wg_ref[:, sl])
        u = _dot(xn, wu_ref[:, sl])
        a = (g * (1.0 / (1.0 + jnp.exp(-g))) * u).astype(bf16)
        acc = acc + _dot(a, wd_ref[sl, :])
    return x + 0.5 * acc


def _ffn_kernel(x_ref, g_ref, wg_ref, wu_ref, wd_ref, o_ref):
    o_ref[0] = _swiglu_residual(x_ref[0], g_ref, wg_ref, wu_ref, wd_ref)


def _ffn_specs(D):
    return [_const_spec((1, D)), _const_spec((D, D_FF)), _const_spec((D, D_FF)), _const_spec((D_FF, D))]


def _ffn(x, g, wg, wu, wd):
    B, L, D = x.shape
    tok = pl.BlockSpec((1, TM_FFN, D), lambda b, t: (b, t, 0))
    return pl.pallas_call(
        _ffn_kernel,
        grid=(B, L // TM_FFN), in_specs=[tok] + _ffn_specs(D), out_specs=tok,
        out_shape=jax.ShapeDtypeStruct((B, L, D), f32),
        compiler_params=_params("parallel", "parallel"),
        name="ffn",
    )(x, g, wg, wu, wd)


def _proj_kernel(h_ref, g_ref, wg_ref, wa_ref, w2_ref, ba_ref, go_ref,
                 qk_ref, v_ref, r_ref, la_ref, q1_ref, kv1_ref, q4_ref, kv4_ref, q16_ref, kv16_ref,
                 zs_ref, z4_ref):
    un = _rms(h_ref[0], g_ref[...]).astype(bf16)
    ngrp = Z_ATT_W // LANES
    per = MXU_N // LANES
    n4 = TM // 4

    def att_cols(c):
        z = _dot(un, wa_ref[:, c * MXU_N:(c + 1) * MXU_N])
        if c * MXU_N < ATT_W:
            q1_ref[0, :, c * MXU_N:(c + 1) * MXU_N] = z.astype(bf16)
        else:
            kv1_ref[0, :, c * MXU_N - ATT_W:(c + 1) * MXU_N - ATT_W] = z.astype(bf16)
        for k in range(per):
            zs_ref[c * per + k] = z[:, k * LANES:(k + 1) * LANES]

    def put(q_ref, kv_ref, c, gi, rows):
        col = gi * LANES
        if col < ATT_W:
            q_ref[0, c, :, col:col + LANES] = rows.astype(bf16)
        else:
            kv_ref[0, c, :, col - ATT_W:col - ATT_W + LANES] = rows.astype(bf16)

    def permute4(classes):
        for c in classes:
            for gi in range(ngrp):
                rows = zs_ref[gi, pl.ds(c, n4, stride=4), :]
                z4_ref[gi, c * n4:(c + 1) * n4, :] = rows
                put(q4_ref, kv4_ref, c, gi, rows)

    def permute16(classes):
        for c4 in classes:
            for m in range(4):
                for gi in range(ngrp):
                    put(q16_ref, kv16_ref, c4 + 4 * m, gi, z4_ref[gi, pl.ds(c4 * n4 + m, n4 // 4, stride=4), :])

    def gla_cols(c):
        ref, off = ((qk_ref, 0), (qk_ref, MXU_N), (v_ref, 0), (v_ref, MXU_N), (r_ref, 0), (r_ref, MXU_N))[c]
        z = _dot(un, wg_ref[:, c * MXU_N:(c + 1) * MXU_N])
        if ref is r_ref:
            z = z * (1.0 / (1.0 + jnp.exp(-z))) * go_ref[:, off:off + MXU_N]
        ref[0, :, off:off + MXU_N] = z.astype(bf16)

    lr = _dot(un, wg_ref[:, Z_GLA_W - LR_PAD:])
    hi = lr.astype(bf16).astype(f32)
    packed = hi + pltpu.roll(lr - hi, 2 * GLA_RANK, axis=1) + pltpu.roll(hi, 4 * GLA_RANK, axis=1)
    att_cols(0)
    pre = _dot(packed.astype(bf16), w2_ref[...]) + ba_ref[...]
    att_cols(1)
    att_cols(2)
    la_ref[0] = (jnp.minimum(pre, 0.0) - jnp.log(1.0 + jnp.exp(-jnp.abs(pre)))) * (1.0 / GLA_TAU)
    for c in range(3, Z_ATT_W // MXU_N):
        att_cols(c)
    gla_cols(0)
    permute4(range(0, 2))
    gla_cols(1)
    permute4(range(2, 4))
    for c in range(2, 6):
        gla_cols(c)
        permute16(range(c - 2, c - 1))


def _proj(h, g, w_gla, w_att, w_gate, ba, g_out):
    B, L, D = h.shape
    tok = lambda w: pl.BlockSpec((1, TM, w), lambda b, t: (b, t, 0))
    cls = lambda d, w: pl.BlockSpec((1, d, TM // d, w), lambda b, t: (b, 0, t, 0))
    sds = jax.ShapeDtypeStruct
    out_shape = [sds((B, L, 2 * GLA_QK_W), bf16), sds((B, L, GLA_V_W), bf16), sds((B, L, GLA_V_W), bf16),
                 sds((B, L, 2 * GLA_QK_W), f32),
                 sds((B, L, ATT_W), bf16), sds((B, L, 2 * ATT_W), bf16)]
    out_specs = [tok(2 * GLA_QK_W), tok(GLA_V_W), tok(GLA_V_W), tok(2 * GLA_QK_W), tok(ATT_W), tok(2 * ATT_W)]
    for d in DILATIONS[1:]:
        out_shape += [sds((B, d, L // d, ATT_W), bf16), sds((B, d, L // d, 2 * ATT_W), bf16)]
        out_specs += [cls(d, ATT_W), cls(d, 2 * ATT_W)]
    return pl.pallas_call(
        _proj_kernel,
        grid=(B, L // TM),
        in_specs=[tok(D), _const_spec((1, D)), _const_spec((D, Z_GLA_W)), _const_spec((D, Z_ATT_W)),
                  _const_spec((LR_PAD, 2 * GLA_QK_W)), _const_spec((1, 2 * GLA_QK_W)), _const_spec((1, GLA_V_W))],
        out_specs=out_specs, out_shape=out_shape,
        scratch_shapes=[pltpu.VMEM((Z_ATT_W // LANES, TM, LANES), f32)] * 2,
        compiler_params=_params("parallel", "parallel"),
        name="proj",
    )(h, g, w_gla, w_att, w_gate, ba, g_out)


def _gla_kernel(qk_ref, v_ref, la_ref, *rest, reverse, final):
    if final:
        of_ref, gate_ref, o_ref, st_ref = rest
    else:
        o_ref, st_ref = rest
    C = GLA_CHUNK
    npair = GLA_HEADS // 2
    zero = jnp.zeros((), bf16)

    @pl.when(pl.program_id(1) == 0)
    def _():
        st_ref[...] = jnp.zeros(st_ref.shape, f32)

    row = lax.broadcasted_iota(jnp.int32, (C, C), 0)
    col = lax.broadcasted_iota(jnp.int32, (C, C), 1)
    tri_b = ((col >= row) if reverse else (col <= row)).astype(bf16)
    row2 = lax.broadcasted_iota(jnp.int32, (C, 2 * C), 0)
    col2 = lax.broadcasted_iota(jnp.int32, (C, 2 * C), 1) % C
    tri2 = (col2 >= row2) if reverse else (col2 <= row2)
    half = lax.broadcasted_iota(jnp.int32, (C, LANES), 1) // GLA_DK
    diag = (lax.broadcasted_iota(jnp.int32, (LANES, 2 * GLA_DV), 0) // GLA_DK
            == lax.broadcasted_iota(jnp.int32, (LANES, 2 * GLA_DV), 1) // GLA_DV)
    zv = jnp.zeros((C, GLA_DV), bf16)
    end = 0 if reverse else C - 1
    nchunk = qk_ref.shape[1] // C
    order = list(range(nchunk - 1, -1, -1) if reverse else range(nchunk))
    lanes = lambda p: slice(p * LANES, (p + 1) * LANES)

    def decays(ci):
        rs = slice(ci * C, (ci + 1) * C)
        la_hi, la_lo = _split(la_ref[0, rs, :])
        b = _dot(tri_b, la_hi) + _dot(tri_b, la_lo)
        b_end = b[end:end + 1, :]
        qk = qk_ref[0, rs, :].astype(f32)
        q, k = qk[:, :GLA_QK_W], qk[:, GLA_QK_W:]
        q_dec = (q * jnp.exp(b)).astype(bf16)
        k_inv = (k * jnp.exp(-b)).astype(bf16)
        k_end = (k * jnp.exp(b_end - b)).astype(bf16)
        dcol = [jnp.exp(jnp.broadcast_to(b_end[:, lanes(p)], (8, LANES)).T[:, 0:1]) for p in range(npair)]
        return q_dec, k_inv, k_end, dcol

    def local(ci, q_dec, k_inv, k_end, dcol):
        rs = slice(ci * C, (ci + 1) * C)
        out = []
        for p in range(npair):
            kp = k_inv[:, lanes(p)]
            kst = jnp.concatenate([jnp.where(half == 0, kp, zero), jnp.where(half == 1, kp, zero)], axis=0)
            a = lax.dot_general(q_dec[:, lanes(p)], kst, _NT, preferred_element_type=f32)
            a = jnp.where(tri2, a, 0.0).astype(bf16)
            vs = slice(2 * p * GLA_DV, (2 * p + 2) * GLA_DV)
            v_even, v_odd = v_ref[0, rs, vs][:, :GLA_DV], v_ref[0, rs, vs][:, GLA_DV:]
            vbd = jnp.concatenate([jnp.concatenate([v_even, zv], axis=1),
                                   jnp.concatenate([zv, v_odd], axis=1)], axis=0)
            kv = lax.dot_general(k_end[:, lanes(p)], v_ref[0, rs, vs], _TN, preferred_element_type=f32)
            out.append((_dot(a, vbd), jnp.where(diag, kv, 0.0)))
        return out

    def carry(pairs, q_dec, dcol, st):
        out = []
        for p, (o2, kv) in enumerate(pairs):
            out.append(o2 + _dot(q_dec[:, lanes(p)], st[p].astype(bf16)))
            st[p] = st[p] * dcol[p] + kv
        return out

    def emit(ci, o2s):
        rs = slice(ci * C, (ci + 1) * C)
        for h in range(GLA_HEADS):
            vs = slice(h * GLA_DV, (h + 1) * GLA_DV)
            o = o2s[h // 2][:, (h % 2) * GLA_DV:(h % 2 + 1) * GLA_DV]
            if final:
                o = o + of_ref[0, rs, vs]
                o = o * lax.rsqrt(jnp.mean(o * o, axis=-1, keepdims=True) + EPS)
                o_ref[0, rs, vs] = (o * gate_ref[0, rs, vs].astype(f32)).astype(bf16)
            else:
                o_ref[0, rs, vs] = o

    st = [st_ref[p] for p in range(npair)]
    dc, lc, oc = {}, {}, {}
    for t in range(len(order) + GLA_LAG_L + GLA_LAG_C + GLA_LAG_E):
        if t < len(order):
            dc[t] = decays(order[t])
        n = t - GLA_LAG_L
        if 0 <= n < len(order):
            lc[n] = local(order[n], *dc[n])
        n = t - GLA_LAG_L - GLA_LAG_C
        if 0 <= n < len(order):
            q_dec, _, _, dcol = dc.pop(n)
            oc[n] = carry(lc.pop(n), q_dec, dcol, st)
        n = t - GLA_LAG_L - GLA_LAG_C - GLA_LAG_E
        if 0 <= n < len(order):
            emit(order[n], oc.pop(n))
    for p in range(npair):
        st_ref[p] = st[p]


def _gla(qk, v, la, gate):
    B, L, _ = qk.shape
    nf, nb = L // TL_FWD, L // TL_BWD
    scratch = [pltpu.VMEM((GLA_HEADS // 2, 2 * GLA_DK, 2 * GLA_DV), f32)]
    fwd = lambda w, cb=0: pl.BlockSpec((1, TL_FWD, w), lambda b, j: (b, j, cb))
    bwd = lambda w, cb=0: pl.BlockSpec((1, TL_BWD, w), lambda b, j: (b, nb - 1 - j, cb))
    o_f = pl.pallas_call(
        functools.partial(_gla_kernel, reverse=False, final=False),
        grid=(B, nf),
        in_specs=[fwd(2 * GLA_QK_W), fwd(GLA_V_W), fwd(GLA_QK_W, 0)],
        out_specs=fwd(GLA_V_W), out_shape=jax.ShapeDtypeStruct((B, L, GLA_V_W), f32),
        scratch_shapes=scratch, compiler_params=_params("parallel", "arbitrary"),
        name="gla_fwd",
    )(qk, v, la)
    return pl.pallas_call(
        functools.partial(_gla_kernel, reverse=True, final=True),
        grid=(B, nb),
        in_specs=[bwd(2 * GLA_QK_W), bwd(GLA_V_W), bwd(GLA_QK_W, 1), bwd(GLA_V_W), bwd(GLA_V_W)],
        out_specs=bwd(GLA_V_W), out_shape=jax.ShapeDtypeStruct((B, L, GLA_V_W), bf16),
        scratch_shapes=scratch, compiler_params=_params("parallel", "arbitrary"),
        name="gla_bwd",
    )(qk, v, la, o_f, gate)


def _att_kernel(q_ref, kv_ref, kvp_ref, kvn_ref, o_ref, lse_ref, vt_ref, qt_ref, ot_ref, bias_ref, *, n_seq, dil):
    i = pl.program_id(0)
    R = ATT_RADIUS
    npair = ATT_HEADS // 2

    @pl.when(i == 0)
    def _():
        key = lax.broadcasted_iota(jnp.int32, (KW, 2 * TQ), 0)
        qry = lax.broadcasted_iota(jnp.int32, (KW, 2 * TQ), 1) % TQ
        odd = lax.broadcasted_iota(jnp.int32, (KW, 2 * TQ), 1) >= TQ
        rel = jnp.abs(key - R - qry)
        relf = rel.astype(f32)
        for var in range(4):
            ok = rel <= R
            if var & 1:
                ok = ok & (key >= R)
            if var & 2:
                ok = ok & (key < R + TQ)
            for p in range(npair):
                s_even, s_odd = [2.0 ** (-8.0 * (h + 1) / ATT_HEADS) * dil * LOG2E for h in (2 * p, 2 * p + 1)]
                bias_ref[var * npair + p] = jnp.where(ok, -jnp.where(odd, s_odd, s_even) * relf, NEG)

    vt_ref[...] = jnp.concatenate([kvp_ref[:, ATT_W:], kv_ref[:, ATT_W:], kvn_ref[:, ATT_W:]], axis=0).T
    qt_ref[...] = q_ref[...].T

    def kwin(a, ps):
        lo, hi = a * TQ - R, a * TQ + TQ + R
        parts = [kvp_ref[:, ps]] if lo < 0 else []
        parts.append(kv_ref[max(lo, 0):min(hi, TB), ps])
        if hi > TB:
            parts.append(kvn_ref[:, ps])
        return jnp.concatenate(parts, axis=0) if len(parts) > 1 else parts[0]

    rhalf = lax.broadcasted_iota(jnp.int32, (LANES, TQ), 0) // ATT_DH
    sub = lax.broadcasted_iota(jnp.int32, (8, TQ), 0)
    zero = jnp.zeros((), bf16)

    def tree(op, x):
        parts = [x[r * 8:(r + 1) * 8] for r in range(KW // 8)]
        while len(parts) > 1:
            parts = [op(parts[k], parts[k + 1]) for k in range(0, len(parts), 2)]
        return parts[0]

    nsub = TB // TQ
    var = []
    for a in range(nsub):
        start = i * TB + a * TQ
        var.append((jnp.where((start % n_seq) == 0, 1, 0) + jnp.where(((start + TQ) % n_seq) == 0, 2, 0)) * npair)

    def scores(a, p):
        ps = slice(p * LANES, (p + 1) * LANES)
        qt = qt_ref[ps, a * TQ:(a + 1) * TQ]
        qst = jnp.concatenate([jnp.where(rhalf == 0, qt, zero), jnp.where(rhalf == 1, qt, zero)], axis=1)
        st = _dot(kwin(a, ps), qst) + bias_ref[var[a] + p]
        m = jnp.max(tree(jnp.maximum, st), axis=0, keepdims=True)
        return st, m

    def probs(st, m):
        return (jnp.exp2(st - m).astype(bf16),)

    ones = jnp.ones((16, KW), bf16)

    def values(a, p, pt, m):
        ps = slice(p * LANES, (p + 1) * LANES)
        ot = _dot(jnp.concatenate([vt_ref[ps, a * TQ:a * TQ + KW], ones], axis=0), pt)
        l = ot[LANES:LANES + 1]
        ot = ot[:LANES] * (1.0 / l)
        ot_ref[ps, a * TQ:(a + 1) * TQ] = jnp.where(rhalf == 0, ot[:, :TQ], ot[:, TQ:])
        return m + jnp.log2(l)

    items = [(a, p) for a in range(nsub) for p in range(npair)]
    sc, pr, lse = {}, {}, {}
    for t in range(len(items) + ATT_LAG_P + ATT_LAG_V):
        if t < len(items):
            sc[t] = scores(*items[t])
        if 0 <= t - ATT_LAG_P < len(items):
            pr[t - ATT_LAG_P] = probs(*sc[t - ATT_LAG_P])
        n = t - ATT_LAG_P - ATT_LAG_V
        if 0 <= n < len(items):
            lse[n] = values(*items[n], *pr.pop(n), sc.pop(n)[1])
    for a in range(nsub):
        lse8 = jnp.zeros((8, TQ), f32)
        for p in range(npair):
            row = lse[a * npair + p]
            lse8 = jnp.where(sub == 2 * p, row[:, :TQ], jnp.where(sub == 2 * p + 1, row[:, TQ:], lse8))
        lse_ref[a * TQ:(a + 1) * TQ, :] = jnp.concatenate([lse8, jnp.zeros((LANES - 8, TQ), f32)], axis=0).T
    o_ref[...] = ot_ref[...].T.astype(bf16)


def _att(q, kv, n_seq, dil):
    rows = q.shape[0]
    nhalo = rows // ATT_RADIUS
    per = TB // ATT_RADIUS
    return pl.pallas_call(
        functools.partial(_att_kernel, n_seq=n_seq, dil=dil),
        grid=(rows // TB,),
        in_specs=[pl.BlockSpec((TB, ATT_W), lambda i: (i, 0)),
                  pl.BlockSpec((TB, 2 * ATT_W), lambda i: (i, 0)),
                  pl.BlockSpec((ATT_RADIUS, 2 * ATT_W), lambda i: (jnp.maximum(i * per - 1, 0), 0)),
                  pl.BlockSpec((ATT_RADIUS, 2 * ATT_W), lambda i: (jnp.minimum((i + 1) * per, nhalo - 1), 0))],
        out_specs=[pl.BlockSpec((TB, ATT_W), lambda i: (i, 0)), pl.BlockSpec((TB, LANES), lambda i: (i, 0))],
        out_shape=[jax.ShapeDtypeStruct((rows, ATT_W), bf16), jax.ShapeDtypeStruct((rows, LANES), f32)],
        scratch_shapes=[pltpu.VMEM((ATT_W, TB + 2 * ATT_RADIUS), bf16),
                        pltpu.VMEM((ATT_W, TB), bf16),
                        pltpu.VMEM((ATT_W, TB), f32),
                        pltpu.VMEM((4 * (ATT_HEADS // 2), KW, 2 * TQ), f32)],
        compiler_params=_params("arbitrary"),
        name=f"att_d{dil}",
    )(q, kv, kv, kv)


def _mix_ffn_kernel(h_ref, og_ref, o1_ref, l1_ref, o4_ref, l4_ref, o16_ref, l16_ref, wo_ref,
                    g_ref, wg_ref, wu_ref, wd_ref, gf_ref, out_ref, os_ref, ls_ref, cat_ref):
    ngrp = ATT_W // LANES
    nout = D_MODEL // MXU_N
    cols = lambda c: slice(c * MXU_N, (c + 1) * MXU_N)
    n4 = TM // 4

    og = og_ref[0]
    part = [h_ref[0, :, cols(c)] + _dot(og, wo_ref[:GLA_V_W, cols(c)]) for c in range(nout)]

    for c in range(16):
        c4, m = c % 4, c // 4
        ls_ref[2, pl.ds(c4 * n4 + m, n4 // 4, stride=4), :] = l16_ref[0, c]
        oc = o16_ref[0, c].astype(f32)
        for gi in range(ngrp):
            os_ref[2, gi, pl.ds(c4 * n4 + m, n4 // 4, stride=4), :] = oc[:, gi * LANES:(gi + 1) * LANES]
    for c in range(4):
        ls_ref[0, pl.ds(c, n4, stride=4), :] = l4_ref[0, c]
        ls_ref[1, pl.ds(c, n4, stride=4), :] = ls_ref[2, c * n4:(c + 1) * n4, :]
        oc = o4_ref[0, c].astype(f32)
        for gi in range(ngrp):
            os_ref[0, gi, pl.ds(c, n4, stride=4), :] = oc[:, gi * LANES:(gi + 1) * LANES]
            os_ref[1, gi, pl.ds(c, n4, stride=4), :] = os_ref[2, gi, c * n4:(c + 1) * n4, :]

    l1, l4, l16 = l1_ref[0], ls_ref[0], ls_ref[1]
    m = jnp.maximum(jnp.maximum(l1, l4), l16)
    w1, w4, w16 = jnp.exp2(l1 - m), jnp.exp2(l4 - m), jnp.exp2(l16 - m)
    inv = 1.0 / (w1 + w4 + w16)
    a1, a4, a16 = w1 * inv, w4 * inv, w16 * inv
    head = lax.broadcasted_iota(jnp.int32, (TM, LANES), 1) // ATT_DH
    for gi in range(ngrp):
        idx = head + 2 * gi
        gs = slice(gi * LANES, (gi + 1) * LANES)
        o = (jnp.take_along_axis(a1, idx, axis=1) * o1_ref[0, :, gs].astype(f32)
             + jnp.take_along_axis(a4, idx, axis=1) * os_ref[0, gi]
             + jnp.take_along_axis(a16, idx, axis=1) * os_ref[1, gi])
        cat_ref[:, gs] = o.astype(bf16)
        if gi % 2 == 1:
            ks = slice((gi - 1) * LANES, (gi + 1) * LANES)
            rows = slice(GLA_V_W + ks.start, GLA_V_W + ks.stop)
            part = [part[c] + _dot(cat_ref[:, ks], wo_ref[rows, cols(c)]) for c in range(nout)]
    h2 = jnp.concatenate(part, axis=-1)
    out_ref[0] = _rms(_swiglu_residual(h2, g_ref, wg_ref, wu_ref, wd_ref), gf_ref[...])


def _mix_ffn(h, og, o1, l1, o4, l4, o16, l16, w_out, g, wg, wu, wd, g_final):
    B, L, D = h.shape
    tok = lambda w: pl.BlockSpec((1, TM, w), lambda b, t: (b, t, 0))
    cls = lambda d, w: pl.BlockSpec((1, d, TM // d, w), lambda b, t: (b, 0, t, 0))
    return pl.pallas_call(
        _mix_ffn_kernel,
        grid=(B, L // TM),
        in_specs=[tok(D), tok(GLA_V_W), tok(ATT_W), tok(LANES), cls(4, ATT_W), cls(4, LANES),
                  cls(16, ATT_W), cls(16, LANES), _const_spec((D, D))] + _ffn_specs(D) + [_const_spec((1, D))],
        out_specs=tok(D), out_shape=jax.ShapeDtypeStruct((B, L, D), f32),
        scratch_shapes=[pltpu.VMEM((3, ATT_W // LANES, TM, LANES), f32), pltpu.VMEM((3, TM, LANES), f32),
                        pltpu.VMEM((TM, ATT_W), bf16)],
        compiler_params=_params("parallel", "parallel"),
        name="mix_ffn",
    )(h, og, o1, l1, o4, l4, o16, l16, w_out, g, wg, wu, wd, g_final)


def _prep_layer(g_ffn1, w1_gate, w1_up, w1_down, g_mix, w_in, w_a2_fwd, b_a_fwd, w_a2_bwd, b_a_bwd,
                g_gla_out, w_out, g_ffn2, w2_gate, w2_up, w2_down):
    c = [0]
    for w in (GLA_QK_W, GLA_QK_W, GLA_V_W, GLA_V_W, GLA_RANK, GLA_RANK, ATT_W, ATT_W, ATT_W):
        c.append(c[-1] + w)
    gq, gk, gv, gr, lrf, lrb, aq, ak, av = [w_in[:, c[i]:c[i + 1]] for i in range(9)]
    pad = jnp.zeros((D_MODEL, LR_PAD - 2 * GLA_RANK), f32)
    w_gla = jnp.concatenate([gq * GLA_DK ** -0.5, gk, gv, gr, lrf, lrb, pad], axis=1).astype(bf16)
    w_att = jnp.concatenate([aq * (ATT_DH ** -0.5 * LOG2E), ak, av], axis=1).astype(bf16)
    wa = jnp.zeros((2 * GLA_RANK, 2 * GLA_QK_W), f32)
    wa = wa.at[:GLA_RANK, :GLA_QK_W].set(w_a2_fwd).at[GLA_RANK:, GLA_QK_W:].set(w_a2_bwd)
    wa_hi = wa.astype(bf16)
    wa_lo = (wa - wa_hi.astype(f32)).astype(bf16)
    w_gate = jnp.concatenate([wa_hi, wa_hi, wa_lo, jnp.zeros_like(wa_hi)], axis=0)
    ba = jnp.concatenate([b_a_fwd, b_a_bwd])[None, :]
    return dict(
        ffn1=(g_ffn1[None, :], w1_gate.astype(bf16), w1_up.astype(bf16), w1_down.astype(bf16)),
        proj=(g_mix[None, :], w_gla, w_att, w_gate, ba, g_gla_out[None, :]),
        w_out=w_out.astype(bf16),
        ffn2=(g_ffn2[None, :], w2_gate.astype(bf16), w2_up.astype(bf16), w2_down.astype(bf16)))


def _layer(x, p, g_final):
    B, L, _ = x.shape
    h = _ffn(x, *p["ffn1"])
    qk, v, r, la, q1, kv1, q4, kv4, q16, kv16 = _proj(h, *p["proj"])
    og = _gla(qk, v, la, r)
    outs = []
    for d, q, kv in ((1, q1, kv1), (4, q4, kv4), (16, q16, kv16)):
        o, lse = _att(q.reshape(B * L, ATT_W), kv.reshape(B * L, 2 * ATT_W), L // d, d)
        shape = (B, L) if d == 1 else (B, d, L // d)
        outs += [o.reshape(*shape, ATT_W), lse.reshape(*shape, LANES)]
    return _mix_ffn(h, og, *outs, p["w_out"], *p["ffn2"], g_final)


def kernel(x_prompt, x_sample, g_ffn1, w1_gate, w1_up, w1_down, g_mix, w_in, w_a2_fwd, b_a_fwd, w_a2_bwd, b_a_bwd, g_gla_out, w_out, g_ffn2, w2_gate, w2_up, w2_down, g_final):
    depth = g_ffn1.shape[0]
    assert depth == 1, "final norm is fused into the last layer's second FFN"
    p = _prep_layer(g_ffn1[0], w1_gate[0], w1_up[0], w1_down[0], g_mix[0], w_in[0], w_a2_fwd[0], b_a_fwd[0],
                    w_a2_bwd[0], b_a_bwd[0], g_gla_out[0], w_out[0], g_ffn2[0], w2_gate[0], w2_up[0], w2_down[0])
    gf = g_final[None, :]
    return (_layer(x_prompt, p, gf), _layer(x_sample, p, gf))
```
